```python
import math
import jax, jax.numpy as jnp
from jax import lax
import numpy as np


D_MODEL = 1024
BATCH = 2
SEQ = 16384
DEPTH = 4

N_MIXERS = 2
HEAD_DIM = 64
MEM_LEN = 256
MEM_HEADS = 4
MIX_HEADS = D_MODEL // HEAD_DIM - MEM_HEADS
SWA_Q_HEADS = MIX_HEADS
SWA_KV_HEADS = MIX_HEADS // 4
WINDOW = 128
BLOCK = 128
DIFF_HEADS = MIX_HEADS // 2
DIFF_V_DIM = 2 * HEAD_DIM
N_EXPERTS = 64
TOP_K = 8
N_GROUPS = 8
TOPK_GROUPS = 4
EXPERT_HIDDEN = D_MODEL // 4
SHARED_HIDDEN = D_MODEL // 4
ROUTED_SCALE = 2.5
MOE_BLOCK = 256
DN_ALPHA = (2.0 * DEPTH) ** 0.25
DN_BETA = (8.0 * DEPTH) ** -0.25
LN_EPS = 1e-5
NEG = -1e30

kernel_name = 'hybrid_swa_sink_diffattn_memxattn_moe_deepnorm'


def alibi_slopes(n):
    def pow2_slopes(m):
        start = 2.0 ** (-8.0 / m)
        return [start ** (i + 1) for i in range(m)]
    if (n & (n - 1)) == 0:
        s = pow2_slopes(n)
    else:
        c = 2 ** int(math.floor(math.log2(n)))
        s = pow2_slopes(c) + pow2_slopes(2 * c)[0::2][: n - c]
    return jnp.asarray(np.array(s, dtype=np.float32))


def layer_norm(x, g, b):
    xf = x.astype(jnp.float32)
    mu = jnp.mean(xf, axis=-1, keepdims=True)
    var = jnp.mean(jnp.square(xf - mu), axis=-1, keepdims=True)
    y = (xf - mu) * lax.rsqrt(var + LN_EPS) * g.astype(jnp.float32) + b.astype(jnp.float32)
    return y.astype(x.dtype)


def rms_norm(x, g):
    xf = x.astype(jnp.float32)
    y = xf * lax.rsqrt(jnp.mean(jnp.square(xf), axis=-1, keepdims=True) + LN_EPS)
    return (y * g.astype(jnp.float32)).astype(x.dtype)


def swa_sink_attention(q, k, v, sinks, slopes):
    B, S = q.shape[:2]
    nb = S // BLOCK
    g = SWA_Q_HEADS // SWA_KV_HEADS
    qb = q.reshape(B, nb, BLOCK, SWA_KV_HEADS, g, HEAD_DIM)
    kb = k.reshape(B, nb, BLOCK, SWA_KV_HEADS, HEAD_DIM)
    vb = v.reshape(B, nb, BLOCK, SWA_KV_HEADS, HEAD_DIM)
    shift = lambda t: jnp.concatenate([jnp.zeros_like(t[:, :1]), t[:, :-1]], axis=1)
    kk = jnp.concatenate([shift(kb), kb], axis=2)
    vv = jnp.concatenate([shift(vb), vb], axis=2)
    qi = jnp.arange(BLOCK)[:, None]
    kj = jnp.arange(2 * BLOCK)[None, :]
    dist = qi - kj + BLOCK
    kpos = (jnp.arange(nb)[:, None] - 1) * BLOCK + jnp.arange(2 * BLOCK)[None, :]
    valid = ((dist >= 0) & (dist < WINDOW))[None] & (kpos >= 0)[:, None, :]
    s = jnp.einsum('bnqhgd,bnkhd->bnhgqk', qb, kk).astype(jnp.float32) * (HEAD_DIM ** -0.5)
    s = s - slopes.reshape(SWA_KV_HEADS, g)[:, :, None, None] * dist.astype(jnp.float32)
    s = jnp.where(valid[None, :, None, None], s, NEG)
    sink = sinks.astype(jnp.float32).reshape(SWA_KV_HEADS, g)[None, None, :, :, None, None]
    m = jnp.maximum(jnp.max(s, axis=-1, keepdims=True), sink)
    p = jnp.exp(s - m)
    p = p / (jnp.sum(p, axis=-1, keepdims=True) + jnp.exp(sink - m))
    out = jnp.einsum('bnhgqk,bnkhd->bnqhgd', p.astype(v.dtype), vv)
    return out.reshape(B, S, SWA_Q_HEADS * HEAD_DIM)


def diff_attention(q, k, v, lam, slopes):
    B, S = q.shape[:2]
    nb = S // BLOCK
    qblocks = jnp.swapaxes(q.reshape(B, nb, BLOCK, DIFF_HEADS, 2, HEAD_DIM), 0, 1)
    kpos = jnp.arange(S)

    def one_block(args):
        qblk, n = args
        qpos = n * BLOCK + jnp.arange(BLOCK)
        dist = qpos[:, None] - kpos[None, :]
        s = jnp.einsum('bqhcd,bkhcd->bhcqk', qblk, k).astype(jnp.float32) * (HEAD_DIM ** -0.5)
        s = s - slopes[None, :, None, None, None] * dist.astype(jnp.float32)
        s = jnp.where(dist >= 0, s, NEG)
        p = jax.nn.softmax(s, axis=-1)
        a = p[:, :, 0] - lam * p[:, :, 1]
        return jnp.einsum('bhqk,bkhe->bqhe', a.astype(v.dtype), v)

    out = lax.map(one_block, (qblocks, jnp.arange(nb)))
    return jnp.swapaxes(out, 0, 1).reshape(B, S, DIFF_HEADS, DIFF_V_DIM)


def memory_attention(q, mem, w_kv):
    B, S = q.shape[:2]
    kv = mem @ w_kv
    km, vm = jnp.split(kv, 2, axis=-1)
    km = km.reshape(B, MEM_LEN, MEM_HEADS, HEAD_DIM)
    vm = vm.reshape(B, MEM_LEN, MEM_HEADS, HEAD_DIM)
    s = jnp.einsum('bshd,bmhd->bhsm', q, km).astype(jnp.float32) * (HEAD_DIM ** -0.5)
    p = jax.nn.softmax(s, axis=-1)
    out = jnp.einsum('bhsm,bmhd->bshd', p.astype(vm.dtype), vm)
    return out.reshape(B, S, MEM_HEADS * HEAD_DIM)


def moe_ffn(x, w_router, router_bias, w_gate, w_up, w_down, ws_gate, ws_up, ws_down):
    B, S, D = x.shape
    n = B * S
    xt = x.reshape(n, D)
    aff = jax.nn.sigmoid((xt @ w_router).astype(jnp.float32))
    sel = aff + router_bias.astype(jnp.float32)
    per_group = N_EXPERTS // N_GROUPS
    gscore = jnp.sum(lax.top_k(sel.reshape(n, N_GROUPS, per_group), 2)[0], axis=-1)
    gidx = lax.top_k(gscore, TOPK_GROUPS)[1]
    gmask = jnp.sum(jax.nn.one_hot(gidx, N_GROUPS, dtype=jnp.float32), axis=1) > 0
    emask = jnp.repeat(gmask, per_group, axis=1)
    eidx = lax.top_k(jnp.where(emask, sel, -jnp.inf), TOP_K)[1]
    gates = jnp.take_along_axis(aff, eidx, axis=-1)
    gates = gates / jnp.sum(gates, axis=-1, keepdims=True) * ROUTED_SCALE

    nk = n * TOP_K
    flat_e = eidx.reshape(-1)
    flat_t = jnp.repeat(jnp.arange(n, dtype=jnp.int32), TOP_K)
    flat_g = gates.reshape(-1)
    order = jnp.argsort(flat_e)
    se, st, sg = flat_e[order], flat_t[order], flat_g[order]
    counts = jnp.bincount(flat_e, length=N_EXPERTS)
    padded = (counts + MOE_BLOCK - 1) // MOE_BLOCK * MOE_BLOCK
    starts = jnp.cumsum(counts) - counts
    pends = jnp.cumsum(padded)
    pstarts = pends - padded
    dest = pstarts[se] + (jnp.arange(nk) - starts[se])
    n_rows = (nk + N_EXPERTS * (MOE_BLOCK - 1) + MOE_BLOCK - 1) // MOE_BLOCK * MOE_BLOCK
    n_blocks = n_rows // MOE_BLOCK
    row_t = jnp.full((n_rows,), n, jnp.int32).at[dest].set(st)
    row_g = jnp.zeros((n_rows,), jnp.float32).at[dest].set(sg)
    block_e = jnp.clip(jnp.searchsorted(pends, jnp.arange(n_blocks) * MOE_BLOCK, side='right'), 0, N_EXPERTS - 1)
    x_pad = jnp.concatenate([xt, jnp.zeros((1, D), xt.dtype)], axis=0)

    def expert_block(args):
        rows, e = args
        xb = x_pad[rows]
        h = jax.nn.silu(xb @ w_gate[e]) * (xb @ w_up[e])
        return h @ w_down[e]

    yb = lax.map(expert_block, (row_t.reshape(n_blocks, MOE_BLOCK), block_e)).reshape(n_rows, D)
    routed = jax.ops.segment_sum(yb * row_g[:, None].astype(yb.dtype), row_t, num_segments=n + 1)[:n]
    shared = (jax.nn.silu(xt @ ws_gate) * (xt @ ws_up)) @ ws_down
    return (routed + shared).reshape(B, S, D)


def setup_inputs(seed: int = 0) -> dict:
    key = jax.random.key(seed)
    ks = jax.random.split(key, 24)
    D = D_MODEL
    n_a = (DEPTH + 1) // 2
    n_b = DEPTH // 2
    sd = D ** -0.5
    nrm = lambda k, shape, scale: jax.random.normal(k, shape, jnp.float32) * scale
    qa, kva = SWA_Q_HEADS * HEAD_DIM, SWA_KV_HEADS * HEAD_DIM
    qb, vb = DIFF_HEADS * 2 * HEAD_DIM, DIFF_HEADS * DIFF_V_DIM
    mq = MEM_HEADS * HEAD_DIM
    w_in_a = jnp.concatenate([nrm(ks[0], (n_a, D, qa + kva), sd),
                              nrm(ks[1], (n_a, D, kva), sd * DN_BETA),
                              nrm(ks[2], (n_a, D, mq), sd)], axis=-1)
    w_in_b = jnp.concatenate([nrm(ks[3], (n_b, D, 2 * qb), sd),
                              nrm(ks[4], (n_b, D, vb), sd * DN_BETA),
                              nrm(ks[5], (n_b, D, mq), sd)], axis=-1)
    w_mem_kv = jnp.concatenate([nrm(ks[6], (DEPTH, D, mq), sd),
                                nrm(ks[7], (DEPTH, D, mq), sd * DN_BETA)], axis=-1)
    return {
        'x': nrm(ks[8], (BATCH, SEQ, D), 1.0),
        'mem': nrm(ks[9], (BATCH, MEM_LEN, D), 1.0),
        'w_in_a': w_in_a,
        'sinks': nrm(ks[10], (n_a, SWA_Q_HEADS), 0.5),
        'w_in_b': w_in_b,
        'lambda_qk': nrm(ks[11], (n_b, 4, HEAD_DIM), 0.1),
        'subln_g': 1.0 + nrm(ks[12], (n_b, DIFF_V_DIM), 0.02),
        'w_mem_kv': w_mem_kv,
        'w_out': nrm(ks[13], (DEPTH, D, D), sd * DN_BETA),
        'ln_g': 1.0 + nrm(ks[14], (DEPTH, 2, D), 0.02),
        'ln_b': nrm(ks[15], (DEPTH, 2, D), 0.02),
        'w_router': nrm(ks[16], (DEPTH, D, N_EXPERTS), sd),
        'router_bias': nrm(ks[17], (DEPTH, N_EXPERTS), 0.01),
        'w_gate': nrm(ks[18], (DEPTH, N_EXPERTS, D, EXPERT_HIDDEN), sd),
        'w_up': nrm(ks[19], (DEPTH, N_EXPERTS, D, EXPERT_HIDDEN), sd),
        'w_down': nrm(ks[20], (DEPTH, N_EXPERTS, EXPERT_HIDDEN, D), EXPERT_HIDDEN ** -0.5 * DN_BETA),
        'w_shared_gate': nrm(ks[21], (DEPTH, D, SHARED_HIDDEN), sd),
        'w_shared_up': nrm(ks[22], (DEPTH, D, SHARED_HIDDEN), sd),
        'w_shared_down': nrm(ks[23], (DEPTH, SHARED_HIDDEN, D), SHARED_HIDDEN ** -0.5 * DN_BETA),
    }


def reference(x, mem, w_in_a, sinks, w_in_b, lambda_qk, subln_g, w_mem_kv, w_out, ln_g, ln_b,
              w_router, router_bias, w_gate, w_up, w_down, w_shared_gate, w_shared_up, w_shared_down):
    B, S, D = x.shape
    slopes_a = alibi_slopes(SWA_Q_HEADS)
    slopes_b = alibi_slopes(DIFF_HEADS)
    qa, kva = SWA_Q_HEADS * HEAD_DIM, SWA_KV_HEADS * HEAD_DIM
    qb, vb = DIFF_HEADS * 2 * HEAD_DIM, DIFF_HEADS * DIFF_V_DIM
    for i in range(DEPTH):
        j = i // N_MIXERS
        if i % N_MIXERS == 0:
            h = x @ w_in_a[j]
            q, k, v, q_mem = jnp.split(h, [qa, qa + kva, qa + 2 * kva], axis=-1)
            mix = swa_sink_attention(q.reshape(B, S, SWA_Q_HEADS, HEAD_DIM),
                                     k.reshape(B, S, SWA_KV_HEADS, HEAD_DIM),
                                     v.reshape(B, S, SWA_KV_HEADS, HEAD_DIM),
                                     sinks[j], slopes_a)
        else:
            h = x @ w_in_b[j]
            q, k, v, q_mem = jnp.split(h, [qb, 2 * qb, 2 * qb + vb], axis=-1)
            lam_init = 0.8 - 0.6 * math.exp(-0.3 * i)
            lp = lambda_qk[j].astype(jnp.float32)
            lam = jnp.exp(jnp.sum(lp[0] * lp[1])) - jnp.exp(jnp.sum(lp[2] * lp[3])) + lam_init
            o = diff_attention(q.reshape(B, S, DIFF_HEADS, 2, HEAD_DIM),
                               k.reshape(B, S, DIFF_HEADS, 2, HEAD_DIM),
                               v.reshape(B, S, DIFF_HEADS, DIFF_V_DIM), lam, slopes_b)
            mix = (rms_norm(o, subln_g[j]) * (1.0 - lam_init)).reshape(B, S, vb)
        mem_out = memory_attention(q_mem.reshape(B, S, MEM_HEADS, HEAD_DIM), mem, w_mem_kv[i])
        y = jnp.concatenate([mix, mem_out], axis=-1) @ w_out[i]
        x = layer_norm(DN_ALPHA * x + y, ln_g[i, 0], ln_b[i, 0])
        f = moe_ffn(x, w_router[i], router_bias[i], w_gate[i], w_up[i], w_down[i],
                    w_shared_gate[i], w_shared_up[i], w_shared_down[i])
        x = layer_norm(DN_ALPHA * x + f, ln_g[i, 1], ln_b[i, 1])
    return x
```

```python
import functools
import math

import jax
import jax.numpy as jnp
import numpy as np
from jax import lax
from jax.experimental import pallas as pl
from jax.experimental.pallas import tpu as pltpu

HEAD_DIM = 64
MEM_HEADS = 4
SWA_Q_HEADS = 12
SWA_KV_HEADS = 3
SWA_GROUP = SWA_Q_HEADS // SWA_KV_HEADS
WINDOW = 128
DIFF_HEADS = 6
DIFF_V_DIM = 2 * HEAD_DIM
N_EXPERTS = 64
TOP_K = 8
N_GROUPS = 8
PER_GROUP = N_EXPERTS // N_GROUPS
TOPK_GROUPS = 4
ROUTED_SCALE = 2.5
MOE_BLOCK = 256
DEPTH = 4
DN_ALPHA = (2.0 * DEPTH) ** 0.25
LN_EPS = 1e-5
NEG = -1e30
QK_SCALE = HEAD_DIM ** -0.5

MXU_DTYPE = jnp.bfloat16

PROJ_TM = 512
DIFF_TQ = 1024
ROUTE_T = 512
DISPATCH_T = 256
COMBINE_T = 128
VMEM_LIMIT = 56 * 1024 * 1024


def _alibi_slopes(n):
    def pow2_slopes(m):
        start = 2.0 ** (-8.0 / m)
        return [start ** (i + 1) for i in range(m)]
    if (n & (n - 1)) == 0:
        return pow2_slopes(n)
    c = 2 ** int(math.floor(math.log2(n)))
    return pow2_slopes(c) + pow2_slopes(2 * c)[0::2][: n - c]


def _dot(a, b):
    return jnp.dot(a, b, preferred_element_type=jnp.float32)


def _dot_nt(a, b):
    return lax.dot_general(a, b, (((1,), (1,)), ((), ())), preferred_element_type=jnp.float32)


def _params(sem):
    return pltpu.CompilerParams(dimension_semantics=sem, vmem_limit_bytes=VMEM_LIMIT)


def _layer_norm(z, g, b):
    mu = jnp.mean(z, axis=-1, keepdims=True)
    zc = z - mu
    var = jnp.mean(zc * zc, axis=-1, keepdims=True)
    return zc * lax.rsqrt(var + LN_EPS) * g + b


def _silu(x):
    return x / (1.0 + jnp.exp(-x))


def _proj_kernel(*refs, scales):
    nw = len(scales)
    x_ref = refs[0]
    w_refs = refs[1:1 + nw]
    o_refs = refs[1 + nw:]
    xb = x_ref[...].astype(MXU_DTYPE)
    for w_ref, o_ref, sc in zip(w_refs, o_refs, scales):
        acc = _dot(xb, w_ref[...])
        if sc != 1.0:
            acc = acc * sc
        o_ref[...] = acc.astype(o_ref.dtype)


def _project(x, weights, scales):
    n, d = x.shape
    tm = min(PROJ_TM, n)
    in_specs = [pl.BlockSpec((tm, d), lambda i: (i, 0))]
    in_specs += [pl.BlockSpec(w.shape, lambda i: (0, 0)) for w in weights]
    out_specs = [pl.BlockSpec((tm, w.shape[1]), lambda i: (i, 0)) for w in weights]
    out_shape = [jax.ShapeDtypeStruct((n, w.shape[1]), MXU_DTYPE) for w in weights]
    return pl.pallas_call(
        functools.partial(_proj_kernel, scales=tuple(scales)),
        grid=(n // tm,),
        in_specs=in_specs,
        out_specs=out_specs,
        out_shape=out_shape,
        compiler_params=_params(("parallel",)),
        name="in_proj",
    )(x, *weights)


def _swa_kernel(sinks_ref, q_ref, kp_ref, kc_ref, vp_ref, vc_ref, o_ref, *, slopes):
    i = pl.program_id(1)
    q = q_ref[...]
    kp, kc, vp, vc = kp_ref[...], kc_ref[...], vp_ref[...], vc_ref[...]
    row = lax.broadcasted_iota(jnp.int32, (WINDOW, WINDOW), 0)
    col = lax.broadcasted_iota(jnp.int32, (WINDOW, WINDOW), 1)
    mask_c = col <= row
    mask_p = col > row + jnp.where(i > 0, 0, WINDOW)
    dist_c = (row - col).astype(jnp.float32)
    dist_p = dist_c + float(WINDOW)
    outs = []
    for kv in range(SWA_KV_HEADS):
        ks = slice(kv * HEAD_DIM, (kv + 1) * HEAD_DIM)
        kph, kch, vph, vch = kp[:, ks], kc[:, ks], vp[:, ks], vc[:, ks]
        for g in range(SWA_GROUP):
            h = kv * SWA_GROUP + g
            qh = q[:, h * HEAD_DIM:(h + 1) * HEAD_DIM]
            sp = jnp.where(mask_p, _dot_nt(qh, kph) - slopes[h] * dist_p, NEG)
            sc = jnp.where(mask_c, _dot_nt(qh, kch) - slopes[h] * dist_c, NEG)
            sink = sinks_ref[h]
            m = jnp.maximum(jnp.maximum(jnp.max(sp, axis=-1, keepdims=True),
                                        jnp.max(sc, axis=-1, keepdims=True)), sink)
            pp = jnp.exp(sp - m)
            pc = jnp.exp(sc - m)
            denom = (jnp.sum(pp, axis=-1, keepdims=True) + jnp.sum(pc, axis=-1, keepdims=True)
                     + jnp.exp(sink - m))
            o = _dot(pp.astype(MXU_DTYPE), vph) + _dot(pc.astype(MXU_DTYPE), vch)
            outs.append(o / denom)
    o_ref[...] = jnp.concatenate(outs, axis=-1).astype(o_ref.dtype)


def _swa_attention(q, k, v, sinks, batch, seq):
    n = batch * seq
    nb = seq // WINDOW
    qw, kw = q.shape[1], k.shape[1]
    cur = lambda b, i: (b * nb + i, 0)
    prev = lambda b, i: (b * nb + jnp.maximum(i - 1, 0), 0)
    return pl.pallas_call(
        functools.partial(_swa_kernel, slopes=tuple(_alibi_slopes(SWA_Q_HEADS))),
        grid=(batch, nb),
        in_specs=[
            pl.BlockSpec(memory_space=pltpu.SMEM),
            pl.BlockSpec((WINDOW, qw), cur),
            pl.BlockSpec((WINDOW, kw), prev),
            pl.BlockSpec((WINDOW, kw), cur),
            pl.BlockSpec((WINDOW, kw), prev),
            pl.BlockSpec((WINDOW, kw), cur),
        ],
        out_specs=pl.BlockSpec((WINDOW, qw), cur),
        out_shape=jax.ShapeDtypeStruct((n, qw), MXU_DTYPE),
        compiler_params=_params(("parallel", "parallel")),
        name="swa_attention",
    )(sinks, q, k, k, v, v)


def _diff_kernel(slopes_ref, q_ref, k_ref, v_ref, boff_ref, bdiag_ref, lam_ref, g_ref, o_ref,
                 m_ref, l_ref, acc_ref, *, tq, lam_init):
    h = pl.program_id(1)
    qi = pl.program_id(2)
    slope = slopes_ref[h]
    q = q_ref[...]
    lane = lax.broadcasted_iota(jnp.int32, (1, 2 * HEAD_DIM), 1)
    first = lane < HEAD_DIM

    m_ref[...] = jnp.full(m_ref.shape, NEG, jnp.float32)
    l_ref[...] = jnp.zeros(l_ref.shape, jnp.float32)
    acc_ref[...] = jnp.zeros(acc_ref.shape, jnp.float32)

    def process(ki, bias_ref, shift):
        start = pl.multiple_of(ki * tq, tq)
        k = k_ref[pl.ds(start, tq), :]
        v = v_ref[pl.ds(start, tq), :]
        zero = jnp.zeros_like(k)
        for c in range(2):
            kc = jnp.where(first, k, zero) if c == 0 else jnp.where(first, zero, k)
            u = _dot_nt(q, kc) + bias_ref[0]
            m_old = m_ref[c]
            m_new = jnp.maximum(m_old, jnp.max(u, axis=-1, keepdims=True) + shift)
            p = jnp.exp(u - (m_new - shift))
            alpha = jnp.exp(m_old - m_new)
            l_ref[c] = alpha * l_ref[c] + jnp.sum(p, axis=-1, keepdims=True)
            acc_ref[c] = alpha * acc_ref[c] + _dot(p.astype(MXU_DTYPE), v)
            m_ref[c] = m_new

    def off_diag(ki, carry):
        shift = slope * ((ki - qi) * tq).astype(jnp.float32)
        process(ki, boff_ref, shift)
        return carry

    lax.fori_loop(0, qi, off_diag, 0)
    process(qi, bdiag_ref, jnp.float32(0.0))

    lp = lam_ref[...]
    lam = (jnp.exp(jnp.sum(lp[0:1] * lp[1:2], axis=-1, keepdims=True))
           - jnp.exp(jnp.sum(lp[2:3] * lp[3:4], axis=-1, keepdims=True)) + lam_init)
    o = acc_ref[0] / l_ref[0] - lam * (acc_ref[1] / l_ref[1])
    ms = jnp.mean(o * o, axis=-1, keepdims=True)
    o = o * lax.rsqrt(ms + LN_EPS) * g_ref[...] * (1.0 - lam_init)
    o_ref[...] = o.astype(o_ref.dtype)


def _diff_attention(q, k, v, lam_qk, subln_g, batch, seq, lam_init):
    n = batch * seq
    tq = min(DIFF_TQ, seq)
    nq = seq // tq
    hw = 2 * HEAD_DIM
    slopes = np.asarray(_alibi_slopes(DIFF_HEADS), np.float32)
    rel = (np.arange(tq)[None, :] - np.arange(tq)[:, None]).astype(np.float32)
    boff = jnp.asarray(slopes[:, None, None] * rel[None])
    bdiag = jnp.asarray(np.where(rel[None] <= 0, slopes[:, None, None] * rel[None], NEG).astype(np.float32))
    kernel = functools.partial(_diff_kernel, tq=tq, lam_init=lam_init)
    return pl.pallas_call(
        kernel,
        grid=(batch, DIFF_HEADS, nq),
        in_specs=[
            pl.BlockSpec(memory_space=pltpu.SMEM),
            pl.BlockSpec((tq, hw), lambda b, h, i: (b * nq + i, h)),
            pl.BlockSpec((seq, hw), lambda b, h, i: (b, h)),
            pl.BlockSpec((seq, hw), lambda b, h, i: (b, h)),
            pl.BlockSpec((1, tq, tq), lambda b, h, i: (h, 0, 0)),
            pl.BlockSpec((1, tq, tq), lambda b, h, i: (h, 0, 0)),
            pl.BlockSpec(lam_qk.shape, lambda b, h, i: (0, 0)),
            pl.BlockSpec(subln_g.shape, lambda b, h, i: (0, 0)),
        ],
        out_specs=pl.BlockSpec((tq, hw), lambda b, h, i: (b * nq + i, h)),
        out_shape=jax.ShapeDtypeStruct((n, DIFF_HEADS * hw), MXU_DTYPE),
        scratch_shapes=[
            pltpu.VMEM((2, tq, 1), jnp.float32),
            pltpu.VMEM((2, tq, 1), jnp.float32),
            pltpu.VMEM((2, tq, hw), jnp.float32),
        ],
        compiler_params=_params(("parallel", "parallel", "arbitrary")),
        name="diff_attention",
    )(jnp.asarray(slopes), q, k, v, boff, bdiag, lam_qk, subln_g)


def _memkv_kernel(mem_ref, w_ref, o_ref):
    o_ref[0] = _dot(mem_ref[...].astype(MXU_DTYPE), w_ref[0].astype(MXU_DTYPE)).astype(o_ref.dtype)


def _mem_kv(mem2d, w_mem_kv):
    depth, d, w = w_mem_kv.shape
    rows = mem2d.shape[0]
    return pl.pallas_call(
        _memkv_kernel,
        grid=(depth,),
        in_specs=[pl.BlockSpec((rows, d), lambda l: (0, 0)),
                  pl.BlockSpec((1, d, w), lambda l: (l, 0, 0))],
        out_specs=pl.BlockSpec((1, rows, w), lambda l: (l, 0, 0)),
        out_shape=jax.ShapeDtypeStruct((depth, rows, w), MXU_DTYPE),
        compiler_params=_params(("parallel",)),
        name="mem_kv_proj",
    )(mem2d, w_mem_kv)


def _outproj_kernel(mix_ref, qm_ref, kv_ref, wo_ref, x_ref, g_ref, b_ref, o_ref):
    qm = qm_ref[...]
    kvm = kv_ref[0]
    mw = MEM_HEADS * HEAD_DIM
    outs = []
    for h in range(MEM_HEADS):
        hs = slice(h * HEAD_DIM, (h + 1) * HEAD_DIM)
        s = _dot_nt(qm[:, hs], kvm[:, hs])
        m = jnp.max(s, axis=-1, keepdims=True)
        p = jnp.exp(s - m)
        l = jnp.sum(p, axis=-1, keepdims=True)
        vh = kvm[:, mw + h * HEAD_DIM: mw + (h + 1) * HEAD_DIM]
        outs.append(_dot(p.astype(MXU_DTYPE), vh) / l)
    mem_out = jnp.concatenate(outs, axis=-1).astype(MXU_DTYPE)
    mixw = mix_ref.shape[1]
    y = _dot(mix_ref[...], wo_ref[0:mixw, :]) + _dot(mem_out, wo_ref[mixw:, :])
    z = DN_ALPHA * x_ref[...] + y
    o_ref[...] = _layer_norm(z, g_ref[...], b_ref[...])


def _out_proj(mix, qm, kv_l, w_out, x, ln_g, ln_b, batch, seq):
    n, d = x.shape
    tm = min(PROJ_TM, seq)
    per_b = seq // tm
    mlen, kvw = kv_l.shape[1], kv_l.shape[2]
    row = lambda i: (i, 0)
    whole = lambda i: (0, 0)
    return pl.pallas_call(
        _outproj_kernel,
        grid=(n // tm,),
        in_specs=[
            pl.BlockSpec((tm, mix.shape[1]), row),
            pl.BlockSpec((tm, qm.shape[1]), row),
            pl.BlockSpec((1, mlen, kvw), lambda i: (i // per_b, 0, 0)),
            pl.BlockSpec(w_out.shape, whole),
            pl.BlockSpec((tm, d), row),
            pl.BlockSpec((1, d), whole),
            pl.BlockSpec((1, d), whole),
        ],
        out_specs=pl.BlockSpec((tm, d), row),
        out_shape=jax.ShapeDtypeStruct((n, d), jnp.float32),
        compiler_params=_params(("parallel",)),
        name="out_proj_ln",
    )(mix, qm, kv_l, w_out, x, ln_g, ln_b)


def _split_hi_lo(a):
    hi = a.astype(jnp.bfloat16)
    lo = (a - hi.astype(jnp.float32)).astype(jnp.bfloat16)
    return hi, lo


def _route_kernel(x_ref, w_ref, bias_ref, eid_ref, tri_ref, gate_ref, rank_ref, cnt_ref, carry_ref):
    i = pl.program_id(0)

    @pl.when(i == 0)
    def _():
        carry_ref[...] = jnp.zeros(carry_ref.shape, jnp.float32)

    xh, xl = _split_hi_lo(x_ref[...])
    wh, wl = _split_hi_lo(w_ref[...])
    logits = _dot_nt(wh, xh) + _dot_nt(wh, xl) + _dot_nt(wl, xh)
    aff = 1.0 / (1.0 + jnp.exp(-logits))
    sel = aff + bias_ref[...]
    t = sel.shape[1]

    sel3 = sel.reshape(PER_GROUP, N_GROUPS, t)
    jidx = lax.broadcasted_iota(jnp.int32, (PER_GROUP, N_GROUPS, t), 0)
    m1 = jnp.max(sel3, axis=0, keepdims=True)
    first = jnp.min(jnp.where(sel3 == m1, jidx, PER_GROUP), axis=0, keepdims=True)
    m2 = jnp.max(jnp.where(jidx == first, -jnp.inf, sel3), axis=0, keepdims=True)
    gscore = (m1 + m2)[0]

    gid = lax.broadcasted_iota(jnp.int32, (N_GROUPS, t), 0)
    grank = jnp.zeros((N_GROUPS, t), jnp.int32)
    for g2 in range(N_GROUPS):
        other = gscore[g2:g2 + 1, :]
        beats = jnp.logical_or(other > gscore, jnp.logical_and(other == gscore, g2 < gid))
        grank = grank + beats.astype(jnp.int32)
    gmask = grank < TOPK_GROUPS
    emask = jnp.broadcast_to(gmask[None], (PER_GROUP, N_GROUPS, t)).reshape(N_EXPERTS, t)
    val = jnp.where(emask, sel, -jnp.inf)

    eid = eid_ref[...]
    erank = jnp.zeros((N_EXPERTS, t), jnp.int32)
    for r2 in range(N_EXPERTS):
        other = val[r2:r2 + 1, :]
        e2 = (r2 % N_GROUPS) * PER_GROUP + r2 // N_GROUPS
        beats = jnp.logical_or(other > val, jnp.logical_and(other == val, e2 < eid))
        erank = erank + beats.astype(jnp.int32)
    chosen = jnp.logical_and(erank < TOP_K, emask)
    picked = jnp.where(chosen, aff, 0.0)
    gate_ref[...] = picked / jnp.sum(picked, axis=0, keepdims=True) * ROUTED_SCALE

    ind = chosen.astype(jnp.bfloat16)
    prefix = _dot(ind, tri_ref[...])
    carry = carry_ref[...]
    rank_ref[...] = jnp.where(chosen, prefix + carry, -1.0)
    carry = carry + jnp.sum(chosen.astype(jnp.float32), axis=1, keepdims=True)
    carry_ref[...] = carry
    cnt_ref[...] = carry


def _route(x, w_router_t, bias_col, eid_col):
    n, d = x.shape
    t = min(ROUTE_T, n)
    tri = jnp.asarray(np.triu(np.ones((t, t), np.float32), 1), jnp.bfloat16)
    whole = lambda i: (0, 0)
    return pl.pallas_call(
        _route_kernel,
        grid=(n // t,),
        in_specs=[
            pl.BlockSpec((t, d), lambda i: (i, 0)),
            pl.BlockSpec((N_EXPERTS, d), whole),
            pl.BlockSpec((N_EXPERTS, 1), whole),
            pl.BlockSpec((N_EXPERTS, 1), whole),
            pl.BlockSpec((t, t), whole),
        ],
        out_specs=[
            pl.BlockSpec((N_EXPERTS, t), lambda i: (0, i)),
            pl.BlockSpec((N_EXPERTS, t), lambda i: (0, i)),
            pl.BlockSpec((N_EXPERTS, 1), whole),
        ],
        out_shape=[
            jax.ShapeDtypeStruct((N_EXPERTS, n), jnp.float32),
            jax.ShapeDtypeStruct((N_EXPERTS, n), jnp.float32),
            jax.ShapeDtypeStruct((N_EXPERTS, 1), jnp.float32),
        ],
        scratch_shapes=[pltpu.VMEM((N_EXPERTS, 1), jnp.float32)],
        compiler_params=_params(("arbitrary",)),
        name="moe_route",
    )(x, w_router_t, bias_col, eid_col, tri)


def _dest_kernel(gate_ref, rank_ref, pstart_ref, ltri_ref, dest_ref, g_ref):
    rank = rank_ref[...]
    chosen = rank >= 0.0
    slot = _dot(ltri_ref[...], chosen.astype(jnp.bfloat16))
    row = rank + pstart_ref[...]
    gate = gate_ref[...]
    dests, gates = [], []
    for k in range(TOP_K):
        hit = jnp.logical_and(chosen, slot == float(k))
        dests.append(jnp.sum(jnp.where(hit, row, 0.0), axis=0, keepdims=True))
        gates.append(jnp.sum(jnp.where(hit, gate, 0.0), axis=0, keepdims=True))
    dest_ref[...] = jnp.concatenate(dests, axis=0).astype(jnp.int32)
    g_ref[...] = jnp.concatenate(gates, axis=0)


def _dest(gate_dense, rank_dense, pstart_col):
    e, n = gate_dense.shape
    t = min(ROUTE_T, n)
    ltri = jnp.asarray(np.tril(np.ones((e, e), np.float32), -1), jnp.bfloat16)
    whole = lambda i: (0, 0)
    tile = lambda i: (0, i)
    return pl.pallas_call(
        _dest_kernel,
        grid=(n // t,),
        in_specs=[pl.BlockSpec((e, t), tile), pl.BlockSpec((e, t), tile),
                  pl.BlockSpec((e, 1), whole), pl.BlockSpec((e, e), whole)],
        out_specs=[pl.BlockSpec((TOP_K, t), tile), pl.BlockSpec((TOP_K, t), tile)],
        out_shape=[jax.ShapeDtypeStruct((TOP_K, n), jnp.int32),
                   jax.ShapeDtypeStruct((TOP_K, n), jnp.float32)],
        compiler_params=_params(("parallel",)),
        name="moe_dest",
    )(gate_dense, rank_dense, pstart_col, ltri)


def _dispatch_kernel(dest_ref, padlo_ref, padn_ref, x_ref, xs_ref, zero_ref, sem, zsem, *, t_tile):
    i = pl.program_id(0)

    @pl.when(i == 0)
    def _():
        zero_ref[...] = jnp.zeros(zero_ref.shape, zero_ref.dtype)

        def zero_row(lo):
            return pltpu.make_async_copy(zero_ref, xs_ref.at[pl.ds(lo, 1), :], zsem)

        def fill(e, carry):
            lo = padlo_ref[e]

            def start(r, c):
                zero_row(lo + r).start()
                return c

            def wait(r, c):
                zero_row(lo + r).wait()
                return c

            lax.fori_loop(0, padn_ref[e], start, 0)
            lax.fori_loop(0, padn_ref[e], wait, 0)
            return carry

        lax.fori_loop(0, N_EXPERTS, fill, 0)

    def issue(t, carry):
        for k in range(TOP_K):
            d = dest_ref[k, t]
            pltpu.make_async_copy(x_ref.at[pl.ds(t, 1), :], xs_ref.at[pl.ds(d, 1), :], sem).start()
        return carry

    lax.fori_loop(0, t_tile, issue, 0)
    for k in range(TOP_K):
        pltpu.make_async_copy(x_ref, xs_ref.at[pl.ds(0, t_tile), :], sem).wait()


def _dispatch(x, dest, pad_lo, pad_n, n_rows):
    n, d = x.shape
    t = min(DISPATCH_T, n)
    return pl.pallas_call(
        functools.partial(_dispatch_kernel, t_tile=t),
        grid=(n // t,),
        in_specs=[
            pl.BlockSpec((TOP_K, t), lambda i: (0, i), memory_space=pltpu.SMEM),
            pl.BlockSpec(memory_space=pltpu.SMEM),
            pl.BlockSpec(memory_space=pltpu.SMEM),
            pl.BlockSpec((t, d), lambda i: (i, 0)),
        ],
        out_specs=pl.BlockSpec(memory_space=pl.ANY),
        out_shape=jax.ShapeDtypeStruct((n_rows, d), jnp.float32),
        scratch_shapes=[pltpu.VMEM((1, d), jnp.float32),
                        pltpu.SemaphoreType.DMA(()), pltpu.SemaphoreType.DMA(())],
        compiler_params=_params(("arbitrary",)),
        name="moe_dispatch",
    )(dest, pad_lo, pad_n, x)


def _experts_kernel(be_ref, nu_ref, xs_ref, wg_ref, wu_ref, wd_ref, y_ref):
    b = pl.program_id(0)

    @pl.when(b < nu_ref[0])
    def _():
        xb = xs_ref[...].astype(MXU_DTYPE)
        g = _dot(xb, wg_ref[0])
        u = _dot(xb, wu_ref[0])
        hmid = (_silu(g) * u).astype(MXU_DTYPE)
        y_ref[...] = _dot(hmid, wd_ref[0])


def _experts(xs, block_e, n_used, w_gate, w_up, w_down):
    n_rows, d = xs.shape
    hid = w_gate.shape[2]
    nblk = n_rows // MOE_BLOCK
    blk = lambda b, be, nu: (jnp.minimum(b, nu[0] - 1), 0)
    grid_spec = pltpu.PrefetchScalarGridSpec(
        num_scalar_prefetch=2,
        grid=(nblk,),
        in_specs=[
            pl.BlockSpec((MOE_BLOCK, d), blk),
            pl.BlockSpec((1, d, hid), lambda b, be, nu: (be[b], 0, 0)),
            pl.BlockSpec((1, d, hid), lambda b, be, nu: (be[b], 0, 0)),
            pl.BlockSpec((1, hid, d), lambda b, be, nu: (be[b], 0, 0)),
        ],
        out_specs=pl.BlockSpec((MOE_BLOCK, d), blk),
    )
    return pl.pallas_call(
        _experts_kernel,
        grid_spec=grid_spec,
        out_shape=jax.ShapeDtypeStruct((n_rows, d), jnp.float32),
        compiler_params=_params(("arbitrary",)),
        name="moe_experts",
    )(block_e, n_used, xs, w_gate, w_up, w_down)


def _combine_kernel(dest_ref, g_ref, x_ref, wsg_ref, wsu_ref, wsd_ref, lg_ref, lb_ref, y_ref, o_ref,
                    buf_ref, sem, *, t_tile):
    def issue(t, carry):
        for k in range(TOP_K):
            d = dest_ref[k, t]
            pltpu.make_async_copy(y_ref.at[pl.ds(d, 1), :], buf_ref.at[k, pl.ds(t, 1), :], sem).start()
        return carry

    lax.fori_loop(0, t_tile, issue, 0)

    x = x_ref[...]
    xb = x.astype(MXU_DTYPE)
    hmid = (_silu(_dot(xb, wsg_ref[...])) * _dot(xb, wsu_ref[...])).astype(MXU_DTYPE)
    z = DN_ALPHA * x + _dot(hmid, wsd_ref[...])

    for k in range(TOP_K):
        pltpu.make_async_copy(y_ref.at[pl.ds(0, t_tile), :], buf_ref.at[k], sem).wait()
    gt = jnp.transpose(g_ref[...])
    for k in range(TOP_K):
        z = z + gt[:, k:k + 1] * buf_ref[k]
    o_ref[...] = _layer_norm(z, lg_ref[...], lb_ref[...])


def _combine(x, dest, gates, y, ws_gate, ws_up, ws_down, ln_g, ln_b):
    n, d = x.shape
    t = min(COMBINE_T, n)
    whole = lambda i: (0, 0)
    tile = lambda i: (0, i)
    return pl.pallas_call(
        functools.partial(_combine_kernel, t_tile=t),
        grid=(n // t,),
        in_specs=[
            pl.BlockSpec((TOP_K, t), tile, memory_space=pltpu.SMEM),
            pl.BlockSpec((TOP_K, t), tile),
            pl.BlockSpec((t, d), lambda i: (i, 0)),
            pl.BlockSpec(ws_gate.shape, whole),
            pl.BlockSpec(ws_up.shape, whole),
            pl.BlockSpec(ws_down.shape, whole),
            pl.BlockSpec((1, d), whole),
            pl.BlockSpec((1, d), whole),
            pl.BlockSpec(memory_space=pl.ANY),
        ],
        out_specs=pl.BlockSpec((t, d), lambda i: (i, 0)),
        out_shape=jax.ShapeDtypeStruct((n, d), jnp.float32),
        scratch_shapes=[pltpu.VMEM((TOP_K, t, d), jnp.float32), pltpu.SemaphoreType.DMA(())],
        compiler_params=_params(("arbitrary",)),
        name="moe_combine",
    )(dest, gates, x, ws_gate, ws_up, ws_down, ln_g, ln_b, y)


def _moe(x, w_router_t, bias_col, eid_col, inv_perm, w_gate, w_up, w_down, ws_gate, ws_up, ws_down,
         ln_g, ln_b):
    n, d = x.shape
    gate_dense, rank_dense, counts = _route(x, w_router_t, bias_col, eid_col)
    cnt = counts[:, 0].astype(jnp.int32)[inv_perm]
    padded = (cnt + MOE_BLOCK - 1) // MOE_BLOCK * MOE_BLOCK
    pends = jnp.cumsum(padded)
    pstarts = pends - padded
    nk = n * TOP_K
    n_rows = (nk + N_EXPERTS * (MOE_BLOCK - 1) + MOE_BLOCK - 1) // MOE_BLOCK * MOE_BLOCK
    nblk = n_rows // MOE_BLOCK
    blk_start = jnp.arange(nblk, dtype=jnp.int32) * MOE_BLOCK
    block_e = jnp.minimum(jnp.sum(blk_start[:, None] >= pends[None, :], axis=1), N_EXPERTS - 1).astype(jnp.int32)
    n_used = (pends[-1:] // MOE_BLOCK).astype(jnp.int32)
    perm = jnp.argsort(inv_perm)
    pstart_col = pstarts[perm].astype(jnp.float32)[:, None]
    dest, gates = _dest(gate_dense, rank_dense, pstart_col)
    xs = _dispatch(x, dest, (pstarts + cnt).astype(jnp.int32), (padded - cnt).astype(jnp.int32), n_rows)
    y = _experts(xs, block_e, n_used, w_gate, w_up, w_down)
    return _combine(x, dest, gates, y, ws_gate, ws_up, ws_down, ln_g, ln_b)


def kernel(x, mem, w_in_a, sinks, w_in_b, lambda_qk, subln_g, w_mem_kv, w_out, ln_g, ln_b, w_router,
           router_bias, w_gate, w_up, w_down, w_shared_gate, w_shared_up, w_shared_down):
    batch, seq, d = x.shape
    n = batch * seq
    cdt = MXU_DTYPE
    qa, kva = SWA_Q_HEADS * HEAD_DIM, SWA_KV_HEADS * HEAD_DIM
    qb, vb = DIFF_HEADS * 2 * HEAD_DIM, DIFF_HEADS * DIFF_V_DIM

    rows = np.arange(N_EXPERTS)
    perm = (rows % N_GROUPS) * PER_GROUP + rows // N_GROUPS
    inv_perm = jnp.asarray(np.argsort(perm), jnp.int32)
    eid_col = jnp.asarray(perm[:, None], jnp.int32)

    xf = x.reshape(n, d)
    kv_all = _mem_kv(mem.reshape(-1, d), w_mem_kv)
    mlen = mem.shape[1]
    kv_all = kv_all.reshape(DEPTH * batch, mlen, kv_all.shape[-1])

    for i in range(DEPTH):
        j = i // 2
        if i % 2 == 0:
            w = w_in_a[j].astype(cdt)
            ws = [w[:, :qa], w[:, qa:qa + kva], w[:, qa + kva:qa + 2 * kva], w[:, qa + 2 * kva:]]
            q, k, v, qm = _project(xf, ws, (QK_SCALE, 1.0, 1.0, QK_SCALE))
            mix = _swa_attention(q, k, v, sinks[j], batch, seq)
        else:
            w = w_in_b[j].astype(cdt)
            ws = [w[:, :qb], w[:, qb:2 * qb], w[:, 2 * qb:2 * qb + vb], w[:, 2 * qb + vb:]]
            q, k, v, qm = _project(xf, ws, (QK_SCALE, 1.0, 1.0, QK_SCALE))
            lam_init = 0.8 - 0.6 * math.exp(-0.3 * i)
            mix = _diff_attention(q, k, v, lambda_qk[j], subln_g[j][None, :], batch, seq, lam_init)
        kv_l = kv_all[i * batch:(i + 1) * batch]
        xf = _out_proj(mix, qm, kv_l, w_out[i].astype(cdt), xf, ln_g[i, 0][None, :], ln_b[i, 0][None, :],
                       batch, seq)
        w_router_t = jnp.transpose(w_router[i])[perm]
        bias_col = router_bias[i][perm][:, None]
        xf = _moe(xf, w_router_t, bias_col, eid_col, inv_perm,
                  w_gate[i].astype(cdt), w_up[i].astype(cdt), w_down[i].astype(cdt),
                  w_shared_gate[i].astype(cdt), w_shared_up[i].astype(cdt), w_shared_down[i].astype(cdt),
                  ln_g[i, 1][None, :], ln_b[i, 1][None, :])
    return xf.reshape(batch, seq, d)
```

```python
import functools
import math

import jax
import jax.numpy as jnp
import numpy as np
from jax import lax
from jax.experimental import pallas as pl
from jax.experimental.pallas import tpu as pltpu

HEAD_DIM = 64
MEM_HEADS = 4
SWA_Q_HEADS = 12
SWA_KV_HEADS = 3
SWA_GROUP = SWA_Q_HEADS // SWA_KV_HEADS
WINDOW = 128
DIFF_HEADS = 6
DIFF_V_DIM = 2 * HEAD_DIM
N_EXPERTS = 64
TOP_K = 8
N_GROUPS = 8
PER_GROUP = N_EXPERTS // N_GROUPS
TOPK_GROUPS = 4
ROUTED_SCALE = 2.5
MOE_BLOCK = 256
DEPTH = 4
DN_ALPHA = (2.0 * DEPTH) ** 0.25
LN_EPS = 1e-5
NEG = -1e30
QK_SCALE = HEAD_DIM ** -0.5

MXU_DTYPE = jnp.bfloat16

PROJ_TM = 512
DIFF_TQ = 1024
ROUTE_T = 512
DISPATCH_T = 256
COMBINE_T = 128
VMEM_LIMIT = 56 * 1024 * 1024


def _alibi_slopes(n):
    def pow2_slopes(m):
        start = 2.0 ** (-8.0 / m)
        return [start ** (i + 1) for i in range(m)]
    if (n & (n - 1)) == 0:
        return pow2_slopes(n)
    c = 2 ** int(math.floor(math.log2(n)))
    return pow2_slopes(c) + pow2_slopes(2 * c)[0::2][: n - c]


def _dot(a, b):
    return jnp.dot(a, b, preferred_element_type=jnp.float32)


def _dot_nt(a, b):
    return lax.dot_general(a, b, (((1,), (1,)), ((), ())), preferred_element_type=jnp.float32)


def _params(sem):
    return pltpu.CompilerParams(dimension_semantics=sem, vmem_limit_bytes=VMEM_LIMIT)


def _layer_norm(z, g, b):
    mu = jnp.mean(z, axis=-1, keepdims=True)
    zc = z - mu
    var = jnp.mean(zc * zc, axis=-1, keepdims=True)
    return zc * lax.rsqrt(var + LN_EPS) * g + b


def _silu(x):
    return x / (1.0 + jnp.exp(-x))


def _proj_kernel(*refs, scales):
    nw = len(scales)
    x_ref = refs[0]
    w_refs = refs[1:1 + nw]
    o_refs = refs[1 + nw:]
    xb = x_ref[...].astype(MXU_DTYPE)
    for w_ref, o_ref, sc in zip(w_refs, o_refs, scales):
        acc = _dot(xb, w_ref[...])
        if sc != 1.0:
            acc = acc * sc
        o_ref[...] = acc.astype(o_ref.dtype)


def _project(x, weights, scales):
    n, d = x.shape
    tm = min(PROJ_TM, n)
    in_specs = [pl.BlockSpec((tm, d), lambda i: (i, 0))]
    in_specs += [pl.BlockSpec(w.shape, lambda i: (0, 0)) for w in weights]
    out_specs = [pl.BlockSpec((tm, w.shape[1]), lambda i: (i, 0)) for w in weights]
    out_shape = [jax.ShapeDtypeStruct((n, w.shape[1]), MXU_DTYPE) for w in weights]
    return pl.pallas_call(
        functools.partial(_proj_kernel, scales=tuple(scales)),
        grid=(n // tm,),
        in_specs=in_specs,
        out_specs=out_specs,
        out_shape=out_shape,
        compiler_params=_params(("parallel",)),
        name="in_proj",
    )(x, *weights)


def _swa_kernel(sinks_ref, q_ref, kp_ref, kc_ref, vp_ref, vc_ref, o_ref, *, slopes):
    i = pl.program_id(1)
    q = q_ref[...]
    kp, kc, vp, vc = kp_ref[...], kc_ref[...], vp_ref[...], vc_ref[...]
    row = lax.broadcasted_iota(jnp.int32, (WINDOW, WINDOW), 0)
    col = lax.broadcasted_iota(jnp.int32, (WINDOW, WINDOW), 1)
    mask_c = col <= row
    mask_p = col > row + jnp.where(i > 0, 0, WINDOW)
    dist_c = (row - col).astype(jnp.float32)
    dist_p = dist_c + float(WINDOW)
    outs = []
    for kv in range(SWA_KV_HEADS):
        ks = slice(kv * HEAD_DIM, (kv + 1) * HEAD_DIM)
        kph, kch, vph, vch = kp[:, ks], kc[:, ks], vp[:, ks], vc[:, ks]
        for g in range(SWA_GROUP):
            h = kv * SWA_GROUP + g
            qh = q[:, h * HEAD_DIM:(h + 1) * HEAD_DIM]
            sp = jnp.where(mask_p, _dot_nt(qh, kph) - slopes[h] * dist_p, NEG)
            sc = jnp.where(mask_c, _dot_nt(qh, kch) - slopes[h] * dist_c, NEG)
            sink = sinks_ref[h]
            m = jnp.maximum(jnp.maximum(jnp.max(sp, axis=-1, keepdims=True),
                                        jnp.max(sc, axis=-1, keepdims=True)), sink)
            pp = jnp.exp(sp - m)
            pc = jnp.exp(sc - m)
            denom = (jnp.sum(pp, axis=-1, keepdims=True) + jnp.sum(pc, axis=-1, keepdims=True)
                     + jnp.exp(sink - m))
            o = _dot(pp.astype(MXU_DTYPE), vph) + _dot(pc.astype(MXU_DTYPE), vch)
            outs.append(o / denom)
    o_ref[...] = jnp.concatenate(outs, axis=-1).astype(o_ref.dtype)


def _swa_attention(q, k, v, sinks, batch, seq):
    n = batch * seq
    nb = seq // WINDOW
    qw, kw = q.shape[1], k.shape[1]
    cur = lambda b, i: (b * nb + i, 0)
    prev = lambda b, i: (b * nb + jnp.maximum(i - 1, 0), 0)
    return pl.pallas_call(
        functools.partial(_swa_kernel, slopes=tuple(_alibi_slopes(SWA_Q_HEADS))),
        grid=(batch, nb),
        in_specs=[
            pl.BlockSpec(memory_space=pltpu.SMEM),
            pl.BlockSpec((WINDOW, qw), cur),
            pl.BlockSpec((WINDOW, kw), prev),
            pl.BlockSpec((WINDOW, kw), cur),
            pl.BlockSpec((WINDOW, kw), prev),
            pl.BlockSpec((WINDOW, kw), cur),
        ],
        out_specs=pl.BlockSpec((WINDOW, qw), cur),
        out_shape=jax.ShapeDtypeStruct((n, qw), MXU_DTYPE),
        compiler_params=_params(("parallel", "parallel")),
        name="swa_attention",
    )(sinks, q, k, k, v, v)


LOG2E = math.log2(math.e)
_BIAS_PIECES = 3


def _bf16_pieces(c):
    out, rest = [], np.float64(c)
    for _ in range(_BIAS_PIECES):
        piece = np.asarray(rest, np.float32).astype(jnp.bfloat16).astype(np.float64)
        out.append(float(piece))
        rest = rest - piece
    return out


def _alibi_templates(tq):
    idx = np.arange(tq)
    hi, lo = (idx // 256) * 256, idx % 256
    hw = 2 * HEAD_DIM
    qt = np.zeros((DIFF_HEADS, 2, tq, hw), np.float32)
    kt = np.zeros((DIFF_HEADS, 2, tq, hw), np.float32)
    for h, slope in enumerate(_alibi_slopes(DIFF_HEADS)):
        for c in range(2):
            base = HEAD_DIM if c == 0 else 0
            for p, piece in enumerate(_bf16_pieces(slope * LOG2E)):
                qt[h, c, :, base + p] = -hi
                kt[h, c, :, base + p] = piece
                qt[h, c, :, base + _BIAS_PIECES + p] = -lo
                kt[h, c, :, base + _BIAS_PIECES + p] = piece
                qt[h, c, :, base + 2 * _BIAS_PIECES + p] = piece
                kt[h, c, :, base + 2 * _BIAS_PIECES + p] = hi
                qt[h, c, :, base + 3 * _BIAS_PIECES + p] = piece
                kt[h, c, :, base + 3 * _BIAS_PIECES + p] = lo
    return jnp.asarray(qt, jnp.bfloat16), jnp.asarray(kt, jnp.bfloat16)


def _diff_kernel(slopes_ref, q_ref, k_ref, v_ref, qt_ref, kt_ref, ones_ref, dmask_ref, lam_ref, g_ref,
                 o_ref, qa_ref, m_ref, acc_ref, *, tq, lam_init):
    h = pl.program_id(1)
    qi = pl.program_id(2)
    slope2 = slopes_ref[h]
    hw = 2 * HEAD_DIM
    lane = lax.broadcasted_iota(jnp.int32, (1, hw), 1)
    first = lane < HEAD_DIM
    q = q_ref[...]
    qa_ref[0] = jnp.where(first, q, qt_ref[0, 0])
    qa_ref[1] = jnp.where(first, qt_ref[0, 1], q)
    m_ref[...] = jnp.full(m_ref.shape, NEG, jnp.float32)
    acc_ref[...] = jnp.zeros(acc_ref.shape, jnp.float32)
    ones = ones_ref[...]

    def softmax_pv(c, u, vaug, shift):
        m_old = m_ref[c]
        m_new = jnp.maximum(m_old, jnp.max(u, axis=-1, keepdims=True) + shift)
        p = jnp.exp2(u - (m_new - shift)).astype(MXU_DTYPE)
        alpha = jnp.exp2(m_old - m_new)
        acc_ref[c] = alpha * acc_ref[c] + _dot(p, vaug)
        m_ref[c] = m_new

    def process(ki, shift, diagonal):
        start = pl.multiple_of(ki * tq, tq)
        k = k_ref[pl.ds(start, tq), :]
        v = v_ref[pl.ds(start, tq), :]
        vaug = jnp.concatenate([v, ones], axis=1)
        kas = (jnp.where(first, k, kt_ref[0, 0]), jnp.where(first, kt_ref[0, 1], k))
        us = [_dot_nt(qa_ref[c], kas[c]) for c in range(2)]
        if diagonal:
            us = [u + dmask_ref[...] for u in us]
        for c in range(2):
            softmax_pv(c, us[c], vaug, shift)

    def off_diag(ki, carry):
        process(ki, slope2 * ((ki - qi) * tq).astype(jnp.float32), False)
        return carry

    lax.fori_loop(0, qi, off_diag, 0)
    process(qi, jnp.float32(0.0), True)

    lp = lam_ref[...]
    lam = (jnp.exp(jnp.sum(lp[0:1] * lp[1:2], axis=-1, keepdims=True))
           - jnp.exp(jnp.sum(lp[2:3] * lp[3:4], axis=-1, keepdims=True)) + lam_init)
    a0, a1 = acc_ref[0], acc_ref[1]
    o = a0[:, :hw] / a0[:, hw:hw + 1] - lam * (a1[:, :hw] / a1[:, hw:hw + 1])
    ms = jnp.mean(o * o, axis=-1, keepdims=True)
    o = o * lax.rsqrt(ms + LN_EPS) * g_ref[...] * (1.0 - lam_init)
    o_ref[...] = o.astype(o_ref.dtype)


def _diff_attention(q, k, v, lam_qk, subln_g, batch, seq, lam_init):
    n = batch * seq
    tq = min(DIFF_TQ, seq)
    nq = seq // tq
    hw = 2 * HEAD_DIM
    slopes2 = np.asarray(_alibi_slopes(DIFF_HEADS), np.float64) * LOG2E
    qt, kt = _alibi_templates(tq)
    ones_col = np.zeros((tq, hw), np.float32)
    ones_col[:, 0] = 1.0
    rel = np.arange(tq)[None, :] - np.arange(tq)[:, None]
    dmask = jnp.asarray(np.where(rel <= 0, 0.0, NEG).astype(np.float32))
    kernel = functools.partial(_diff_kernel, tq=tq, lam_init=lam_init)
    fixed = lambda b, h, i: (0, 0)
    per_head = lambda b, h, i: (h, 0, 0, 0)
    return pl.pallas_call(
        kernel,
        grid=(batch, DIFF_HEADS, nq),
        in_specs=[
            pl.BlockSpec(memory_space=pltpu.SMEM),
            pl.BlockSpec((tq, hw), lambda b, h, i: (b * nq + i, h)),
            pl.BlockSpec((seq, hw), lambda b, h, i: (b, h)),
            pl.BlockSpec((seq, hw), lambda b, h, i: (b, h)),
            pl.BlockSpec((1, 2, tq, hw), per_head),
            pl.BlockSpec((1, 2, tq, hw), per_head),
            pl.BlockSpec((tq, hw), fixed),
            pl.BlockSpec((tq, tq), fixed),
            pl.BlockSpec(lam_qk.shape, fixed),
            pl.BlockSpec(subln_g.shape, fixed),
        ],
        out_specs=pl.BlockSpec((tq, hw), lambda b, h, i: (b * nq + i, h)),
        out_shape=jax.ShapeDtypeStruct((n, DIFF_HEADS * hw), MXU_DTYPE),
        scratch_shapes=[
            pltpu.VMEM((2, tq, hw), MXU_DTYPE),
            pltpu.VMEM((2, tq, 1), jnp.float32),
            pltpu.VMEM((2, tq, 2 * hw), jnp.float32),
        ],
        compiler_params=_params(("parallel", "parallel", "arbitrary")),
        name="diff_attention",
    )(jnp.asarray(slopes2, jnp.float32), q, k, v, qt, kt, jnp.asarray(ones_col, MXU_DTYPE), dmask,
      lam_qk, subln_g)


def _memkv_kernel(mem_ref, w_ref, o_ref):
    o_ref[0] = _dot(mem_ref[...].astype(MXU_DTYPE), w_ref[0].astype(MXU_DTYPE)).astype(o_ref.dtype)


def _mem_kv(mem2d, w_mem_kv):
    depth, d, w = w_mem_kv.shape
    rows = mem2d.shape[0]
    return pl.pallas_call(
        _memkv_kernel,
        grid=(depth,),
        in_specs=[pl.BlockSpec((rows, d), lambda l: (0, 0)),
                  pl.BlockSpec((1, d, w), lambda l: (l, 0, 0))],
        out_specs=pl.BlockSpec((1, rows, w), lambda l: (l, 0, 0)),
        out_shape=jax.ShapeDtypeStruct((depth, rows, w), MXU_DTYPE),
        compiler_params=_params(("parallel",)),
        name="mem_kv_proj",
    )(mem2d, w_mem_kv)


def _outproj_kernel(mix_ref, qm_ref, kv_ref, wo_ref, x_ref, g_ref, b_ref, o_ref):
    qm = qm_ref[...]
    kvm = kv_ref[0]
    mw = MEM_HEADS * HEAD_DIM
    outs = []
    for h in range(MEM_HEADS):
        hs = slice(h * HEAD_DIM, (h + 1) * HEAD_DIM)
        s = _dot_nt(qm[:, hs], kvm[:, hs])
        m = jnp.max(s, axis=-1, keepdims=True)
        p = jnp.exp(s - m)
        l = jnp.sum(p, axis=-1, keepdims=True)
        vh = kvm[:, mw + h * HEAD_DIM: mw + (h + 1) * HEAD_DIM]
        outs.append(_dot(p.astype(MXU_DTYPE), vh) / l)
    mem_out = jnp.concatenate(outs, axis=-1).astype(MXU_DTYPE)
    mixw = mix_ref.shape[1]
    y = _dot(mix_ref[...], wo_ref[0:mixw, :]) + _dot(mem_out, wo_ref[mixw:, :])
    z = DN_ALPHA * x_ref[...] + y
    o_ref[...] = _layer_norm(z, g_ref[...], b_ref[...])


def _out_proj(mix, qm, kv_l, w_out, x, ln_g, ln_b, batch, seq):
    n, d = x.shape
    tm = min(PROJ_TM, seq)
    per_b = seq // tm
    mlen, kvw = kv_l.shape[1], kv_l.shape[2]
    row = lambda i: (i, 0)
    whole = lambda i: (0, 0)
    return pl.pallas_call(
        _outproj_kernel,
        grid=(n // tm,),
        in_specs=[
            pl.BlockSpec((tm, mix.shape[1]), row),
            pl.BlockSpec((tm, qm.shape[1]), row),
            pl.BlockSpec((1, mlen, kvw), lambda i: (i // per_b, 0, 0)),
            pl.BlockSpec(w_out.shape, whole),
            pl.BlockSpec((tm, d), row),
            pl.BlockSpec((1, d), whole),
            pl.BlockSpec((1, d), whole),
        ],
        out_specs=pl.BlockSpec((tm, d), row),
        out_shape=jax.ShapeDtypeStruct((n, d), jnp.float32),
        compiler_params=_params(("parallel",)),
        name="out_proj_ln",
    )(mix, qm, kv_l, w_out, x, ln_g, ln_b)


def _split_hi_lo(a):
    hi = a.astype(jnp.bfloat16)
    lo = (a - hi.astype(jnp.float32)).astype(jnp.bfloat16)
    return hi, lo


def _route_kernel(x_ref, w_ref, bias_ref, eid_ref, tri_ref, gate_ref, rank_ref, cnt_ref, carry_ref):
    i = pl.program_id(0)

    @pl.when(i == 0)
    def _():
        carry_ref[...] = jnp.zeros(carry_ref.shape, jnp.float32)

    xh, xl = _split_hi_lo(x_ref[...])
    wh, wl = _split_hi_lo(w_ref[...])
    logits = _dot_nt(wh, xh) + _dot_nt(wh, xl) + _dot_nt(wl, xh)
    aff = 1.0 / (1.0 + jnp.exp(-logits))
    sel = aff + bias_ref[...]
    t = sel.shape[1]

    sel3 = sel.reshape(PER_GROUP, N_GROUPS, t)
    jidx = lax.broadcasted_iota(jnp.int32, (PER_GROUP, N_GROUPS, t), 0)
    m1 = jnp.max(sel3, axis=0, keepdims=True)
    first = jnp.min(jnp.where(sel3 == m1, jidx, PER_GROUP), axis=0, keepdims=True)
    m2 = jnp.max(jnp.where(jidx == first, -jnp.inf, sel3), axis=0, keepdims=True)
    gscore = (m1 + m2)[0]

    gid = lax.broadcasted_iota(jnp.int32, (N_GROUPS, t), 0)
    grank = jnp.zeros((N_GROUPS, t), jnp.int32)
    for g2 in range(N_GROUPS):
        other = gscore[g2:g2 + 1, :]
        beats = jnp.logical_or(other > gscore, jnp.logical_and(other == gscore, g2 < gid))
        grank = grank + beats.astype(jnp.int32)
    gmask = grank < TOPK_GROUPS
    emask = jnp.broadcast_to(gmask[None], (PER_GROUP, N_GROUPS, t)).reshape(N_EXPERTS, t)
    val = jnp.where(emask, sel, -jnp.inf)

    eid = eid_ref[...]
    erank = jnp.zeros((N_EXPERTS, t), jnp.int32)
    for r2 in range(N_EXPERTS):
        other = val[r2:r2 + 1, :]
        e2 = (r2 % N_GROUPS) * PER_GROUP + r2 // N_GROUPS
        beats = jnp.logical_or(other > val, jnp.logical_and(other == val, e2 < eid))
        erank = erank + beats.astype(jnp.int32)
    chosen = jnp.logical_and(erank < TOP_K, emask)
    picked = jnp.where(chosen, aff, 0.0)
    gate_ref[...] = picked / jnp.sum(picked, axis=0, keepdims=True) * ROUTED_SCALE

    ind = chosen.astype(jnp.bfloat16)
    prefix = _dot(ind, tri_ref[...])
    carry = carry_ref[...]
    rank_ref[...] = jnp.where(chosen, prefix + carry, -1.0)
    carry = carry + jnp.sum(chosen.astype(jnp.float32), axis=1, keepdims=True)
    carry_ref[...] = carry
    cnt_ref[...] = carry


def _route(x, w_router_t, bias_col, eid_col):
    n, d = x.shape
    t = min(ROUTE_T, n)
    tri = jnp.asarray(np.triu(np.ones((t, t), np.float32), 1), jnp.bfloat16)
    whole = lambda i: (0, 0)
    return pl.pallas_call(
        _route_kernel,
        grid=(n // t,),
        in_specs=[
            pl.BlockSpec((t, d), lambda i: (i, 0)),
            pl.BlockSpec((N_EXPERTS, d), whole),
            pl.BlockSpec((N_EXPERTS, 1), whole),
            pl.BlockSpec((N_EXPERTS, 1), whole),
            pl.BlockSpec((t, t), whole),
        ],
        out_specs=[
            pl.BlockSpec((N_EXPERTS, t), lambda i: (0, i)),
            pl.BlockSpec((N_EXPERTS, t), lambda i: (0, i)),
            pl.BlockSpec((N_EXPERTS, 1), whole),
        ],
        out_shape=[
            jax.ShapeDtypeStruct((N_EXPERTS, n), jnp.float32),
            jax.ShapeDtypeStruct((N_EXPERTS, n), jnp.float32),
            jax.ShapeDtypeStruct((N_EXPERTS, 1), jnp.float32),
        ],
        scratch_shapes=[pltpu.VMEM((N_EXPERTS, 1), jnp.float32)],
        compiler_params=_params(("arbitrary",)),
        name="moe_route",
    )(x, w_router_t, bias_col, eid_col, tri)


def _dest_kernel(gate_ref, rank_ref, pstart_ref, ltri_ref, dest_ref, g_ref):
    rank = rank_ref[...]
    chosen = rank >= 0.0
    slot = _dot(ltri_ref[...], chosen.astype(jnp.bfloat16))
    row = rank + pstart_ref[...]
    gate = gate_ref[...]
    dests, gates = [], []
    for k in range(TOP_K):
        hit = jnp.logical_and(chosen, slot == float(k))
        dests.append(jnp.sum(jnp.where(hit, row, 0.0), axis=0, keepdims=True))
        gates.append(jnp.sum(jnp.where(hit, gate, 0.0), axis=0, keepdims=True))
    dest_ref[...] = jnp.concatenate(dests, axis=0).astype(jnp.int32)
    g_ref[...] = jnp.concatenate(gates, axis=0)


def _dest(gate_dense, rank_dense, pstart_col):
    e, n = gate_dense.shape
    t = min(ROUTE_T, n)
    ltri = jnp.asarray(np.tril(np.ones((e, e), np.float32), -1), jnp.bfloat16)
    whole = lambda i: (0, 0)
    tile = lambda i: (0, i)
    return pl.pallas_call(
        _dest_kernel,
        grid=(n // t,),
        in_specs=[pl.BlockSpec((e, t), tile), pl.BlockSpec((e, t), tile),
                  pl.BlockSpec((e, 1), whole), pl.BlockSpec((e, e), whole)],
        out_specs=[pl.BlockSpec((TOP_K, t), tile), pl.BlockSpec((TOP_K, t), tile)],
        out_shape=[jax.ShapeDtypeStruct((TOP_K, n), jnp.int32),
                   jax.ShapeDtypeStruct((TOP_K, n), jnp.float32)],
        compiler_params=_params(("parallel",)),
        name="moe_dest",
    )(gate_dense, rank_dense, pstart_col, ltri)


def _dispatch_kernel(dest_ref, padlo_ref, padn_ref, x_ref, xs_ref, zero_ref, sem, zsem, *, t_tile):
    i = pl.program_id(0)

    @pl.when(i == 0)
    def _():
        zero_ref[...] = jnp.zeros(zero_ref.shape, zero_ref.dtype)

        def zero_row(lo):
            return pltpu.make_async_copy(zero_ref, xs_ref.at[pl.ds(lo, 1), :], zsem)

        def fill(e, carry):
            lo = padlo_ref[e]

            def start(r, c):
                zero_row(lo + r).start()
                return c

            def wait(r, c):
                zero_row(lo + r).wait()
                return c

            lax.fori_loop(0, padn_ref[e], start, 0)
            lax.fori_loop(0, padn_ref[e], wait, 0)
            return carry

        lax.fori_loop(0, N_EXPERTS, fill, 0)

    def issue(t, carry):
        for k in range(TOP_K):
            d = dest_ref[k, t]
            pltpu.make_async_copy(x_ref.at[pl.ds(t, 1), :], xs_ref.at[pl.ds(d, 1), :], sem).start()
        return carry

    lax.fori_loop(0, t_tile, issue, 0)
    for k in range(TOP_K):
        pltpu.make_async_copy(x_ref, xs_ref.at[pl.ds(0, t_tile), :], sem).wait()


def _dispatch(x, dest, pad_lo, pad_n, n_rows):
    n, d = x.shape
    t = min(DISPATCH_T, n)
    return pl.pallas_call(
        functools.partial(_dispatch_kernel, t_tile=t),
        grid=(n // t,),
        in_specs=[
            pl.BlockSpec((TOP_K, t), lambda i: (0, i), memory_space=pltpu.SMEM),
            pl.BlockSpec(memory_space=pltpu.SMEM),
            pl.BlockSpec(memory_space=pltpu.SMEM),
            pl.BlockSpec((t, d), lambda i: (i, 0)),
        ],
        out_specs=pl.BlockSpec(memory_space=pl.ANY),
        out_shape=jax.ShapeDtypeStruct((n_rows, d), jnp.float32),
        scratch_shapes=[pltpu.VMEM((1, d), jnp.float32),
                        pltpu.SemaphoreType.DMA(()), pltpu.SemaphoreType.DMA(())],
        compiler_params=_params(("arbitrary",)),
        name="moe_dispatch",
    )(dest, pad_lo, pad_n, x)


def _experts_kernel(be_ref, nu_ref, xs_ref, wg_ref, wu_ref, wd_ref, y_ref):
    b = pl.program_id(0)

    @pl.when(b < nu_ref[0])
    def _():
        xb = xs_ref[...].astype(MXU_DTYPE)
        g = _dot(xb, wg_ref[0])
        u = _dot(xb, wu_ref[0])
        hmid = (_silu(g) * u).astype(MXU_DTYPE)
        y_ref[...] = _dot(hmid, wd_ref[0])

    @pl.when(b >= nu_ref[0])
    def _():
        y_ref[...] = jnp.zeros(y_ref.shape, y_ref.dtype)


def _experts(xs, block_e, n_used, w_gate, w_up, w_down):
    n_rows, d = xs.shape
    hid = w_gate.shape[2]
    nblk = n_rows // MOE_BLOCK
    blk = lambda b, be, nu: (b, 0)
    used_blk = lambda b, be, nu: (jnp.minimum(b, nu[0] - 1), 0)
    grid_spec = pltpu.PrefetchScalarGridSpec(
        num_scalar_prefetch=2,
        grid=(nblk,),
        in_specs=[
            pl.BlockSpec((MOE_BLOCK, d), used_blk),
            pl.BlockSpec((1, d, hid), lambda b, be, nu: (be[b], 0, 0)),
            pl.BlockSpec((1, d, hid), lambda b, be, nu: (be[b], 0, 0)),
            pl.BlockSpec((1, hid, d), lambda b, be, nu: (be[b], 0, 0)),
        ],
        out_specs=pl.BlockSpec((MOE_BLOCK, d), blk),
    )
    return pl.pallas_call(
        _experts_kernel,
        grid_spec=grid_spec,
        out_shape=jax.ShapeDtypeStruct((n_rows, d), jnp.float32),
        compiler_params=_params(("arbitrary",)),
        name="moe_experts",
    )(block_e, n_used, xs, w_gate, w_up, w_down)


def _combine_kernel(dest_ref, g_ref, x_ref, wsg_ref, wsu_ref, wsd_ref, lg_ref, lb_ref, y_ref, o_ref,
                    buf_ref, sem, *, t_tile):
    def issue(t, carry):
        for k in range(TOP_K):
            d = dest_ref[k, t]
            pltpu.make_async_copy(y_ref.at[pl.ds(d, 1), :], buf_ref.at[k, pl.ds(t, 1), :], sem).start()
        return carry

    lax.fori_loop(0, t_tile, issue, 0)

    x = x_ref[...]
    xb = x.astype(MXU_DTYPE)
    hmid = (_silu(_dot(xb, wsg_ref[...])) * _dot(xb, wsu_ref[...])).astype(MXU_DTYPE)
    z = DN_ALPHA * x + _dot(hmid, wsd_ref[...])

    for k in range(TOP_K):
        pltpu.make_async_copy(y_ref.at[pl.ds(0, t_tile), :], buf_ref.at[k], sem).wait()
    gt = jnp.transpose(g_ref[...])
    for k in range(TOP_K):
        z = z + gt[:, k:k + 1] * buf_ref[k]
    o_ref[...] = _layer_norm(z, lg_ref[...], lb_ref[...])


def _combine(x, dest, gates, y, ws_gate, ws_up, ws_down, ln_g, ln_b):
    n, d = x.shape
    t = min(COMBINE_T, n)
    whole = lambda i: (0, 0)
    tile = lambda i: (0, i)
    return pl.pallas_call(
        functools.partial(_combine_kernel, t_tile=t),
        grid=(n // t,),
        in_specs=[
            pl.BlockSpec((TOP_K, t), tile, memory_space=pltpu.SMEM),
            pl.BlockSpec((TOP_K, t), tile),
            pl.BlockSpec((t, d), lambda i: (i, 0)),
            pl.BlockSpec(ws_gate.shape, whole),
            pl.BlockSpec(ws_up.shape, whole),
            pl.BlockSpec(ws_down.shape, whole),
            pl.BlockSpec((1, d), whole),
            pl.BlockSpec((1, d), whole),
            pl.BlockSpec(memory_space=pl.ANY),
        ],
        out_specs=pl.BlockSpec((t, d), lambda i: (i, 0)),
        out_shape=jax.ShapeDtypeStruct((n, d), jnp.float32),
        scratch_shapes=[pltpu.VMEM((TOP_K, t, d), jnp.float32), pltpu.SemaphoreType.DMA(())],
        compiler_params=_params(("arbitrary",)),
        name="moe_combine",
    )(dest, gates, x, ws_gate, ws_up, ws_down, ln_g, ln_b, y)


def _moe(x, w_router_t, bias_col, eid_col, inv_perm, w_gate, w_up, w_down, ws_gate, ws_up, ws_down,
         ln_g, ln_b):
    n, d = x.shape
    gate_dense, rank_dense, counts = _route(x, w_router_t, bias_col, eid_col)
    cnt = counts[:, 0].astype(jnp.int32)[inv_perm]
    padded = (cnt + MOE_BLOCK - 1) // MOE_BLOCK * MOE_BLOCK
    pends = jnp.cumsum(padded)
    pstarts = pends - padded
    nk = n * TOP_K
    n_rows = (nk + N_EXPERTS * (MOE_BLOCK - 1) + MOE_BLOCK - 1) // MOE_BLOCK * MOE_BLOCK
    nblk = n_rows // MOE_BLOCK
    blk_start = jnp.arange(nblk, dtype=jnp.int32) * MOE_BLOCK
    block_e = jnp.minimum(jnp.sum(blk_start[:, None] >= pends[None, :], axis=1), N_EXPERTS - 1).astype(jnp.int32)
    n_used = (pends[-1:] // MOE_BLOCK).astype(jnp.int32)
    perm = jnp.argsort(inv_perm)
    pstart_col = pstarts[perm].astype(jnp.float32)[:, None]
    dest, gates = _dest(gate_dense, rank_dense, pstart_col)
    xs = _dispatch(x, dest, (pstarts + cnt).astype(jnp.int32), (padded - cnt).astype(jnp.int32), n_rows)
    y = _experts(xs, block_e, n_used, w_gate, w_up, w_down)
    return _combine(x, dest, gates, y, ws_gate, ws_up, ws_down, ln_g, ln_b)


def kernel(x, mem, w_in_a, sinks, w_in_b, lambda_qk, subln_g, w_mem_kv, w_out, ln_g, ln_b, w_router,
           router_bias, w_gate, w_up, w_down, w_shared_gate, w_shared_up, w_shared_down):
    batch, seq, d = x.shape
    n = batch * seq
    cdt = MXU_DTYPE
    qa, kva = SWA_Q_HEADS * HEAD_DIM, SWA_KV_HEADS * HEAD_DIM
    qb, vb = DIFF_HEADS * 2 * HEAD_DIM, DIFF_HEADS * DIFF_V_DIM

    rows = np.arange(N_EXPERTS)
    perm = (rows % N_GROUPS) * PER_GROUP + rows // N_GROUPS
    inv_perm = jnp.asarray(np.argsort(perm), jnp.int32)
    eid_col = jnp.asarray(perm[:, None], jnp.int32)

    xf = x.reshape(n, d)
    kv_all = _mem_kv(mem.reshape(-1, d), w_mem_kv)
    mlen = mem.shape[1]
    kv_all = kv_all.reshape(DEPTH * batch, mlen, kv_all.shape[-1])

    for i in range(DEPTH):
        j = i // 2
        if i % 2 == 0:
            w = w_in_a[j].astype(cdt)
            ws = [w[:, :qa], w[:, qa:qa + kva], w[:, qa + kva:qa + 2 * kva], w[:, qa + 2 * kva:]]
            q, k, v, qm = _project(xf, ws, (QK_SCALE, 1.0, 1.0, QK_SCALE))
            mix = _swa_attention(q, k, v, sinks[j], batch, seq)
        else:
            w = w_in_b[j].astype(cdt)
            ws = [w[:, :qb], w[:, qb:2 * qb], w[:, 2 * qb:2 * qb + vb], w[:, 2 * qb + vb:]]
            q, k, v, qm = _project(xf, ws, (QK_SCALE * LOG2E, 1.0, 1.0, QK_SCALE))
            lam_init = 0.8 - 0.6 * math.exp(-0.3 * i)
            mix = _diff_attention(q, k, v, lambda_qk[j], subln_g[j][None, :], batch, seq, lam_init)
        kv_l = kv_all[i * batch:(i + 1) * batch]
        xf = _out_proj(mix, qm, kv_l, w_out[i].astype(cdt), xf, ln_g[i, 0][None, :], ln_b[i, 0][None, :],
                       batch, seq)
        w_router_t = jnp.transpose(w_router[i])[perm]
        bias_col = router_bias[i][perm][:, None]
        xf = _moe(xf, w_router_t, bias_col, eid_col, inv_perm,
                  w_gate[i].astype(cdt), w_up[i].astype(cdt), w_down[i].astype(cdt),
                  w_shared_gate[i].astype(cdt), w_shared_up[i].astype(cdt), w_shared_down[i].astype(cdt),
                  ln_g[i, 1][None, :], ln_b[i, 1][None, :])
    return xf.reshape(batch, seq, d)
```

```python
import functools
import math

import jax
import jax.numpy as jnp
import numpy as np
from jax import lax
from jax.experimental import pallas as pl
from jax.experimental.pallas import tpu as pltpu

HEAD_DIM = 64
MEM_HEADS = 4
SWA_Q_HEADS = 12
SWA_KV_HEADS = 3
SWA_GROUP = SWA_Q_HEADS // SWA_KV_HEADS
WINDOW = 128
DIFF_HEADS = 6
DIFF_V_DIM = 2 * HEAD_DIM
N_EXPERTS = 64
TOP_K = 8
N_GROUPS = 8
PER_GROUP = N_EXPERTS // N_GROUPS
TOPK_GROUPS = 4
ROUTED_SCALE = 2.5
MOE_BLOCK = 256
DEPTH = 4
DN_ALPHA = (2.0 * DEPTH) ** 0.25
LN_EPS = 1e-5
NEG = -1e30
QK_SCALE = HEAD_DIM ** -0.5

MXU_DTYPE = jnp.bfloat16

PROJ_TM = 512
DIFF_TQ = 1024
ROUTE_T = 512
DISPATCH_T = 256
COMBINE_T = 128
VMEM_LIMIT = 56 * 1024 * 1024


def _alibi_slopes(n):
    def pow2_slopes(m):
        start = 2.0 ** (-8.0 / m)
        return [start ** (i + 1) for i in range(m)]
    if (n & (n - 1)) == 0:
        return pow2_slopes(n)
    c = 2 ** int(math.floor(math.log2(n)))
    return pow2_slopes(c) + pow2_slopes(2 * c)[0::2][: n - c]


def _dot(a, b):
    return jnp.dot(a, b, preferred_element_type=jnp.float32)


def _dot_nt(a, b):
    return lax.dot_general(a, b, (((1,), (1,)), ((), ())), preferred_element_type=jnp.float32)


def _params(sem):
    return pltpu.CompilerParams(dimension_semantics=sem, vmem_limit_bytes=VMEM_LIMIT)


def _layer_norm(z, g, b):
    mu = jnp.mean(z, axis=-1, keepdims=True)
    zc = z - mu
    var = jnp.mean(zc * zc, axis=-1, keepdims=True)
    return zc * lax.rsqrt(var + LN_EPS) * g + b


def _silu(x):
    return x / (1.0 + jnp.exp(-x))


def _proj_kernel(*refs, scales):
    nw = len(scales)
    x_ref = refs[0]
    w_refs = refs[1:1 + nw]
    o_refs = refs[1 + nw:]
    xb = x_ref[...].astype(MXU_DTYPE)
    for w_ref, o_ref, sc in zip(w_refs, o_refs, scales):
        acc = _dot(xb, w_ref[...])
        if sc != 1.0:
            acc = acc * sc
        o_ref[...] = acc.astype(o_ref.dtype)


def _project(x, weights, scales):
    n, d = x.shape
    tm = min(PROJ_TM, n)
    in_specs = [pl.BlockSpec((tm, d), lambda i: (i, 0))]
    in_specs += [pl.BlockSpec(w.shape, lambda i: (0, 0)) for w in weights]
    out_specs = [pl.BlockSpec((tm, w.shape[1]), lambda i: (i, 0)) for w in weights]
    out_shape = [jax.ShapeDtypeStruct((n, w.shape[1]), MXU_DTYPE) for w in weights]
    return pl.pallas_call(
        functools.partial(_proj_kernel, scales=tuple(scales)),
        grid=(n // tm,),
        in_specs=in_specs,
        out_specs=out_specs,
        out_shape=out_shape,
        compiler_params=_params(("parallel",)),
        name="in_proj",
    )(x, *weights)


def _swa_kernel(sinks_ref, q_ref, kp_ref, kc_ref, vp_ref, vc_ref, o_ref, *, slopes):
    i = pl.program_id(1)
    q = q_ref[...]
    kp, kc, vp, vc = kp_ref[...], kc_ref[...], vp_ref[...], vc_ref[...]
    row = lax.broadcasted_iota(jnp.int32, (WINDOW, WINDOW), 0)
    col = lax.broadcasted_iota(jnp.int32, (WINDOW, WINDOW), 1)
    mask_c = col <= row
    mask_p = col > row + jnp.where(i > 0, 0, WINDOW)
    dist_c = (row - col).astype(jnp.float32)
    dist_p = dist_c + float(WINDOW)
    outs = []
    for kv in range(SWA_KV_HEADS):
        ks = slice(kv * HEAD_DIM, (kv + 1) * HEAD_DIM)
        kph, kch, vph, vch = kp[:, ks], kc[:, ks], vp[:, ks], vc[:, ks]
        for g in range(SWA_GROUP):
            h = kv * SWA_GROUP + g
            qh = q[:, h * HEAD_DIM:(h + 1) * HEAD_DIM]
            sp = jnp.where(mask_p, _dot_nt(qh, kph) - slopes[h] * dist_p, NEG)
            sc = jnp.where(mask_c, _dot_nt(qh, kch) - slopes[h] * dist_c, NEG)
            sink = sinks_ref[h]
            m = jnp.maximum(jnp.maximum(jnp.max(sp, axis=-1, keepdims=True),
                                        jnp.max(sc, axis=-1, keepdims=True)), sink)
            pp = jnp.exp(sp - m)
            pc = jnp.exp(sc - m)
            denom = (jnp.sum(pp, axis=-1, keepdims=True) + jnp.sum(pc, axis=-1, keepdims=True)
                     + jnp.exp(sink - m))
            o = _dot(pp.astype(MXU_DTYPE), vph) + _dot(pc.astype(MXU_DTYPE), vch)
            outs.append(o / denom)
    o_ref[...] = jnp.concatenate(outs, axis=-1).astype(o_ref.dtype)


def _swa_attention(q, k, v, sinks, batch, seq):
    n = batch * seq
    nb = seq // WINDOW
    qw, kw = q.shape[1], k.shape[1]
    cur = lambda b, i: (b * nb + i, 0)
    prev = lambda b, i: (b * nb + jnp.maximum(i - 1, 0), 0)
    return pl.pallas_call(
        functools.partial(_swa_kernel, slopes=tuple(_alibi_slopes(SWA_Q_HEADS))),
        grid=(batch, nb),
        in_specs=[
            pl.BlockSpec(memory_space=pltpu.SMEM),
            pl.BlockSpec((WINDOW, qw), cur),
            pl.BlockSpec((WINDOW, kw), prev),
            pl.BlockSpec((WINDOW, kw), cur),
            pl.BlockSpec((WINDOW, kw), prev),
            pl.BlockSpec((WINDOW, kw), cur),
        ],
        out_specs=pl.BlockSpec((WINDOW, qw), cur),
        out_shape=jax.ShapeDtypeStruct((n, qw), MXU_DTYPE),
        compiler_params=_params(("parallel", "parallel")),
        name="swa_attention",
    )(sinks, q, k, k, v, v)


LOG2E = math.log2(math.e)
_BIAS_PIECES = 3


def _bf16_pieces(c):
    out, rest = [], np.float64(c)
    for _ in range(_BIAS_PIECES):
        piece = np.asarray(rest, np.float32).astype(jnp.bfloat16).astype(np.float64)
        out.append(float(piece))
        rest = rest - piece
    return out


def _alibi_templates(tq):
    idx = np.arange(tq)
    hi, lo = (idx // 256) * 256, idx % 256
    hw = 2 * HEAD_DIM
    qt = np.zeros((DIFF_HEADS, 2, tq, hw), np.float32)
    kt = np.zeros((DIFF_HEADS, 2, tq, hw), np.float32)
    for h, slope in enumerate(_alibi_slopes(DIFF_HEADS)):
        for c in range(2):
            base = HEAD_DIM if c == 0 else 0
            for p, piece in enumerate(_bf16_pieces(slope * LOG2E)):
                qt[h, c, :, base + p] = -hi
                kt[h, c, :, base + p] = piece
                qt[h, c, :, base + _BIAS_PIECES + p] = -lo
                kt[h, c, :, base + _BIAS_PIECES + p] = piece
                qt[h, c, :, base + 2 * _BIAS_PIECES + p] = piece
                kt[h, c, :, base + 2 * _BIAS_PIECES + p] = hi
                qt[h, c, :, base + 3 * _BIAS_PIECES + p] = piece
                kt[h, c, :, base + 3 * _BIAS_PIECES + p] = lo
    return jnp.asarray(qt, jnp.bfloat16), jnp.asarray(kt, jnp.bfloat16)


def _diff_kernel(slopes_ref, q_ref, k_ref, v_ref, qt_ref, kt_ref, ones_ref, dmask_ref, lam_ref, g_ref,
                 o_ref, qa_ref, m_ref, acc_ref, *, tq, lam_init):
    h = pl.program_id(1)
    qi = pl.program_id(2)
    slope2 = slopes_ref[h]
    hw = 2 * HEAD_DIM
    lane = lax.broadcasted_iota(jnp.int32, (1, hw), 1)
    first = lane < HEAD_DIM
    q = q_ref[...]
    qa_ref[0] = jnp.where(first, q, qt_ref[0, 0])
    qa_ref[1] = jnp.where(first, qt_ref[0, 1], q)
    m_ref[...] = jnp.full(m_ref.shape, NEG, jnp.float32)
    acc_ref[...] = jnp.zeros(acc_ref.shape, jnp.float32)
    ones = ones_ref[...]

    def softmax_pv(c, u, vaug, shift):
        m_old = m_ref[c]
        m_new = jnp.maximum(m_old, jnp.max(u, axis=-1, keepdims=True) + shift)
        p = jnp.exp2(u - (m_new - shift)).astype(MXU_DTYPE)
        alpha = jnp.exp2(m_old - m_new)
        acc_ref[c] = alpha * acc_ref[c] + _dot(p, vaug)
        m_ref[c] = m_new

    def process(ki, shift, diagonal):
        start = pl.multiple_of(ki * tq, tq)
        k = k_ref[pl.ds(start, tq), :]
        v = v_ref[pl.ds(start, tq), :]
        vaug = jnp.concatenate([v, ones], axis=1)
        kas = (jnp.where(first, k, kt_ref[0, 0]), jnp.where(first, kt_ref[0, 1], k))
        us = [_dot_nt(qa_ref[c], kas[c]) for c in range(2)]
        if diagonal:
            us = [u + dmask_ref[...] for u in us]
        for c in range(2):
            softmax_pv(c, us[c], vaug, shift)

    def off_diag(ki, carry):
        process(ki, slope2 * ((ki - qi) * tq).astype(jnp.float32), False)
        return carry

    lax.fori_loop(0, qi, off_diag, 0)
    process(qi, jnp.float32(0.0), True)

    lp = lam_ref[...]
    lam = (jnp.exp(jnp.sum(lp[0:1] * lp[1:2], axis=-1, keepdims=True))
           - jnp.exp(jnp.sum(lp[2:3] * lp[3:4], axis=-1, keepdims=True)) + lam_init)
    a0, a1 = acc_ref[0], acc_ref[1]
    o = a0[:, :hw] / a0[:, hw:hw + 1] - lam * (a1[:, :hw] / a1[:, hw:hw + 1])
    ms = jnp.mean(o * o, axis=-1, keepdims=True)
    o = o * lax.rsqrt(ms + LN_EPS) * g_ref[...] * (1.0 - lam_init)
    o_ref[...] = o.astype(o_ref.dtype)


def _diff_attention(q, k, v, lam_qk, subln_g, batch, seq, lam_init):
    n = batch * seq
    tq = min(DIFF_TQ, seq)
    nq = seq // tq
    hw = 2 * HEAD_DIM
    slopes2 = np.asarray(_alibi_slopes(DIFF_HEADS), np.float64) * LOG2E
    qt, kt = _alibi_templates(tq)
    ones_col = np.zeros((tq, hw), np.float32)
    ones_col[:, 0] = 1.0
    rel = np.arange(tq)[None, :] - np.arange(tq)[:, None]
    dmask = jnp.asarray(np.where(rel <= 0, 0.0, NEG).astype(np.float32))
    kernel = functools.partial(_diff_kernel, tq=tq, lam_init=lam_init)
    fixed = lambda b, h, i: (0, 0)
    per_head = lambda b, h, i: (h, 0, 0, 0)
    return pl.pallas_call(
        kernel,
        grid=(batch, DIFF_HEADS, nq),
        in_specs=[
            pl.BlockSpec(memory_space=pltpu.SMEM),
            pl.BlockSpec((tq, hw), lambda b, h, i: (b * nq + i, h)),
            pl.BlockSpec((seq, hw), lambda b, h, i: (b, h)),
            pl.BlockSpec((seq, hw), lambda b, h, i: (b, h)),
            pl.BlockSpec((1, 2, tq, hw), per_head),
            pl.BlockSpec((1, 2, tq, hw), per_head),
            pl.BlockSpec((tq, hw), fixed),
            pl.BlockSpec((tq, tq), fixed),
            pl.BlockSpec(lam_qk.shape, fixed),
            pl.BlockSpec(subln_g.shape, fixed),
        ],
        out_specs=pl.BlockSpec((tq, hw), lambda b, h, i: (b * nq + i, h)),
        out_shape=jax.ShapeDtypeStruct((n, DIFF_HEADS * hw), MXU_DTYPE),
        scratch_shapes=[
            pltpu.VMEM((2, tq, hw), MXU_DTYPE),
            pltpu.VMEM((2, tq, 1), jnp.float32),
            pltpu.VMEM((2, tq, 2 * hw), jnp.float32),
        ],
        compiler_params=_params(("parallel", "parallel", "arbitrary")),
        name="diff_attention",
    )(jnp.asarray(slopes2, jnp.float32), q, k, v, qt, kt, jnp.asarray(ones_col, MXU_DTYPE), dmask,
      lam_qk, subln_g)


def _memkv_kernel(mem_ref, w_ref, o_ref):
    o_ref[0] = _dot(mem_ref[...].astype(MXU_DTYPE), w_ref[0].astype(MXU_DTYPE)).astype(o_ref.dtype)


def _mem_kv(mem2d, w_mem_kv):
    depth, d, w = w_mem_kv.shape
    rows = mem2d.shape[0]
    return pl.pallas_call(
        _memkv_kernel,
        grid=(depth,),
        in_specs=[pl.BlockSpec((rows, d), lambda l: (0, 0)),
                  pl.BlockSpec((1, d, w), lambda l: (l, 0, 0))],
        out_specs=pl.BlockSpec((1, rows, w), lambda l: (l, 0, 0)),
        out_shape=jax.ShapeDtypeStruct((depth, rows, w), MXU_DTYPE),
        compiler_params=_params(("parallel",)),
        name="mem_kv_proj",
    )(mem2d, w_mem_kv)


ROW_SLAB = 4
_HI16 = 0xFFFF0000


def _bf16_bits_hi(x):
    b = lax.bitcast_convert_type(x, jnp.uint32)
    b = b + jnp.uint32(0x7FFF) + ((b >> 16) & jnp.uint32(1))
    return b & jnp.uint32(_HI16)


def _pack_rows(z, ref, base, t):
    half = z.shape[1] // 2
    words = _bf16_bits_hi(z[:, half:]) | (_bf16_bits_hi(z[:, :half]) >> 16)
    for s in range(ROW_SLAB):
        ref[pl.ds(base * ROW_SLAB + s, t, stride=ROW_SLAB), :] = words[:, s * 128:(s + 1) * 128]


def _unpack_rows(ref, base, t):
    lo, hi = [], []
    for s in range(ROW_SLAB):
        w = ref[pl.ds(base * ROW_SLAB + s, t, stride=ROW_SLAB), :]
        lo.append(lax.bitcast_convert_type(w << 16, jnp.float32))
        hi.append(lax.bitcast_convert_type(w & jnp.uint32(_HI16), jnp.float32))
    return lo + hi


def _outproj_kernel(mix_ref, qm_ref, kv_ref, wo_ref, x_ref, g_ref, b_ref, o_ref, op_ref):
    qm = qm_ref[...]
    kvm = kv_ref[0]
    mw = MEM_HEADS * HEAD_DIM
    outs = []
    for h in range(MEM_HEADS):
        hs = slice(h * HEAD_DIM, (h + 1) * HEAD_DIM)
        s = _dot_nt(qm[:, hs], kvm[:, hs])
        m = jnp.max(s, axis=-1, keepdims=True)
        p = jnp.exp(s - m)
        l = jnp.sum(p, axis=-1, keepdims=True)
        vh = kvm[:, mw + h * HEAD_DIM: mw + (h + 1) * HEAD_DIM]
        outs.append(_dot(p.astype(MXU_DTYPE), vh) / l)
    mem_out = jnp.concatenate(outs, axis=-1).astype(MXU_DTYPE)
    mixw = mix_ref.shape[1]
    y = _dot(mix_ref[...], wo_ref[0:mixw, :]) + _dot(mem_out, wo_ref[mixw:, :])
    z = DN_ALPHA * x_ref[...] + y
    zn = _layer_norm(z, g_ref[...], b_ref[...])
    o_ref[...] = zn
    _pack_rows(zn, op_ref, 0, zn.shape[0])


def _out_proj(mix, qm, kv_l, w_out, x, ln_g, ln_b, batch, seq):
    n, d = x.shape
    tm = min(PROJ_TM, seq)
    per_b = seq // tm
    mlen, kvw = kv_l.shape[1], kv_l.shape[2]
    row = lambda i: (i, 0)
    whole = lambda i: (0, 0)
    xn, xp = pl.pallas_call(
        _outproj_kernel,
        grid=(n // tm,),
        in_specs=[
            pl.BlockSpec((tm, mix.shape[1]), row),
            pl.BlockSpec((tm, qm.shape[1]), row),
            pl.BlockSpec((1, mlen, kvw), lambda i: (i // per_b, 0, 0)),
            pl.BlockSpec(w_out.shape, whole),
            pl.BlockSpec((tm, d), row),
            pl.BlockSpec((1, d), whole),
            pl.BlockSpec((1, d), whole),
        ],
        out_specs=[pl.BlockSpec((tm, d), row), pl.BlockSpec((tm * ROW_SLAB, 128), row)],
        out_shape=[jax.ShapeDtypeStruct((n, d), jnp.float32),
                   jax.ShapeDtypeStruct((n * ROW_SLAB, 128), jnp.uint32)],
        compiler_params=_params(("parallel",)),
        name="out_proj_ln",
    )(mix, qm, kv_l, w_out, x, ln_g, ln_b)
    return xn, xp.reshape(n, ROW_SLAB, 128)


def _split_hi_lo(a):
    hi = a.astype(jnp.bfloat16)
    lo = (a - hi.astype(jnp.float32)).astype(jnp.bfloat16)
    return hi, lo


def _route_kernel(x_ref, w_ref, bias_ref, eid_ref, tri_ref, gate_ref, rank_ref, cnt_ref, carry_ref):
    i = pl.program_id(0)

    @pl.when(i == 0)
    def _():
        carry_ref[...] = jnp.zeros(carry_ref.shape, jnp.float32)

    xh, xl = _split_hi_lo(x_ref[...])
    wh, wl = _split_hi_lo(w_ref[...])
    logits = _dot_nt(wh, xh) + _dot_nt(wh, xl) + _dot_nt(wl, xh)
    aff = 1.0 / (1.0 + jnp.exp(-logits))
    sel = aff + bias_ref[...]
    t = sel.shape[1]

    sel3 = sel.reshape(PER_GROUP, N_GROUPS, t)
    jidx = lax.broadcasted_iota(jnp.int32, (PER_GROUP, N_GROUPS, t), 0)
    m1 = jnp.max(sel3, axis=0, keepdims=True)
    first = jnp.min(jnp.where(sel3 == m1, jidx, PER_GROUP), axis=0, keepdims=True)
    m2 = jnp.max(jnp.where(jidx == first, -jnp.inf, sel3), axis=0, keepdims=True)
    gscore = (m1 + m2)[0]

    gid = lax.broadcasted_iota(jnp.int32, (N_GROUPS, t), 0)
    grank = jnp.zeros((N_GROUPS, t), jnp.int32)
    for g2 in range(N_GROUPS):
        other = gscore[g2:g2 + 1, :]
        beats = jnp.logical_or(other > gscore, jnp.logical_and(other == gscore, g2 < gid))
        grank = grank + beats.astype(jnp.int32)
    gmask = grank < TOPK_GROUPS
    emask = jnp.broadcast_to(gmask[None], (PER_GROUP, N_GROUPS, t)).reshape(N_EXPERTS, t)
    val = jnp.where(emask, sel, -jnp.inf)

    eid = eid_ref[...]
    erank = jnp.zeros((N_EXPERTS, t), jnp.int32)
    for r2 in range(N_EXPERTS):
        other = val[r2:r2 + 1, :]
        e2 = (r2 % N_GROUPS) * PER_GROUP + r2 // N_GROUPS
        beats = jnp.logical_or(other > val, jnp.logical_and(other == val, e2 < eid))
        erank = erank + beats.astype(jnp.int32)
    chosen = jnp.logical_and(erank < TOP_K, emask)
    picked = jnp.where(chosen, aff, 0.0)
    gate_ref[...] = picked / jnp.sum(picked, axis=0, keepdims=True) * ROUTED_SCALE

    ind = chosen.astype(jnp.bfloat16)
    prefix = _dot(ind, tri_ref[...])
    carry = carry_ref[...]
    rank_ref[...] = jnp.where(chosen, prefix + carry, -1.0)
    carry = carry + jnp.sum(chosen.astype(jnp.float32), axis=1, keepdims=True)
    carry_ref[...] = carry
    cnt_ref[...] = carry


def _route(x, w_router_t, bias_col, eid_col):
    n, d = x.shape
    t = min(ROUTE_T, n)
    tri = jnp.asarray(np.triu(np.ones((t, t), np.float32), 1), jnp.bfloat16)
    whole = lambda i: (0, 0)
    return pl.pallas_call(
        _route_kernel,
        grid=(n // t,),
        in_specs=[
            pl.BlockSpec((t, d), lambda i: (i, 0)),
            pl.BlockSpec((N_EXPERTS, d), whole),
            pl.BlockSpec((N_EXPERTS, 1), whole),
            pl.BlockSpec((N_EXPERTS, 1), whole),
            pl.BlockSpec((t, t), whole),
        ],
        out_specs=[
            pl.BlockSpec((N_EXPERTS, t), lambda i: (0, i)),
            pl.BlockSpec((N_EXPERTS, t), lambda i: (0, i)),
            pl.BlockSpec((N_EXPERTS, 1), whole),
        ],
        out_shape=[
            jax.ShapeDtypeStruct((N_EXPERTS, n), jnp.float32),
            jax.ShapeDtypeStruct((N_EXPERTS, n), jnp.float32),
            jax.ShapeDtypeStruct((N_EXPERTS, 1), jnp.float32),
        ],
        scratch_shapes=[pltpu.VMEM((N_EXPERTS, 1), jnp.float32)],
        compiler_params=_params(("arbitrary",)),
        name="moe_route",
    )(x, w_router_t, bias_col, eid_col, tri)


def _dest_kernel(gate_ref, rank_ref, pstart_ref, ltri_ref, dest_ref, g_ref):
    rank = rank_ref[...]
    chosen = rank >= 0.0
    slot = _dot(ltri_ref[...], chosen.astype(jnp.bfloat16))
    row = rank + pstart_ref[...]
    gate = gate_ref[...]
    dests, gates = [], []
    for k in range(TOP_K):
        hit = jnp.logical_and(chosen, slot == float(k))
        dests.append(jnp.sum(jnp.where(hit, row, 0.0), axis=0, keepdims=True))
        gates.append(jnp.sum(jnp.where(hit, gate, 0.0), axis=0, keepdims=True))
    dest_ref[...] = jnp.concatenate(dests, axis=0).astype(jnp.int32)
    g_ref[...] = jnp.concatenate(gates, axis=0)


def _dest(gate_dense, rank_dense, pstart_col):
    e, n = gate_dense.shape
    t = min(ROUTE_T, n)
    ltri = jnp.asarray(np.tril(np.ones((e, e), np.float32), -1), jnp.bfloat16)
    whole = lambda i: (0, 0)
    tile = lambda i: (0, i)
    return pl.pallas_call(
        _dest_kernel,
        grid=(n // t,),
        in_specs=[pl.BlockSpec((e, t), tile), pl.BlockSpec((e, t), tile),
                  pl.BlockSpec((e, 1), whole), pl.BlockSpec((e, e), whole)],
        out_specs=[pl.BlockSpec((TOP_K, t), tile), pl.BlockSpec((TOP_K, t), tile)],
        out_shape=[jax.ShapeDtypeStruct((TOP_K, n), jnp.int32),
                   jax.ShapeDtypeStruct((TOP_K, n), jnp.float32)],
        compiler_params=_params(("parallel",)),
        name="moe_dest",
    )(gate_dense, rank_dense, pstart_col, ltri)


def _dispatch_kernel(dest_ref, padlo_ref, padn_ref, nu_ref, xp_ref, xs_ref, zero_ref, sem, zsem, *,
                     t_tile, nblk):
    i = pl.program_id(0)

    @pl.when(i == 0)
    def _():
        zero_ref[...] = jnp.zeros(zero_ref.shape, zero_ref.dtype)

        def for_each_pad_copy(e, fn):
            off = padlo_ref[e]
            cnt = padn_ref[e]
            for bit in reversed(range(MOE_BLOCK.bit_length() - 1)):
                size = 1 << bit
                on = (cnt & size) != 0

                @pl.when(on)
                def _():
                    fn(pltpu.make_async_copy(zero_ref.at[pl.ds(0, size)], xs_ref.at[pl.ds(off, size)], zsem))
                off = off + jnp.where(on, size, 0)

        def tail_copy(b):
            return pltpu.make_async_copy(zero_ref, xs_ref.at[pl.ds(b * MOE_BLOCK, MOE_BLOCK)], zsem)

        def start_pad(e, c):
            for_each_pad_copy(e, lambda cp: cp.start())
            return c

        def wait_pad(e, c):
            for_each_pad_copy(e, lambda cp: cp.wait())
            return c

        def start_tail(b, c):
            tail_copy(b).start()
            return c

        def wait_tail(b, c):
            tail_copy(b).wait()
            return c

        lax.fori_loop(0, N_EXPERTS, start_pad, 0)
        lax.fori_loop(nu_ref[0], nblk, start_tail, 0)
        lax.fori_loop(0, N_EXPERTS, wait_pad, 0)
        lax.fori_loop(nu_ref[0], nblk, wait_tail, 0)

    def issue(t, carry):
        for k in range(TOP_K):
            d = dest_ref[k, t]
            pltpu.make_async_copy(xp_ref.at[t], xs_ref.at[d], sem).start(priority=k % 2)
        return carry

    lax.fori_loop(0, t_tile, issue, 0)
    for k in range(TOP_K):
        pltpu.make_async_copy(xp_ref, xs_ref.at[pl.ds(0, t_tile)], sem).wait()


def _dispatch(xp, dest, pad_lo, pad_n, n_used, n_rows):
    n = xp.shape[0]
    t = min(DISPATCH_T, n)
    nblk = n_rows // MOE_BLOCK
    smem = pl.BlockSpec(memory_space=pltpu.SMEM)
    return pl.pallas_call(
        functools.partial(_dispatch_kernel, t_tile=t, nblk=nblk),
        grid=(n // t,),
        in_specs=[
            pl.BlockSpec((TOP_K, t), lambda i: (0, i), memory_space=pltpu.SMEM),
            smem, smem, smem,
            pl.BlockSpec((t, ROW_SLAB, 128), lambda i: (i, 0, 0)),
        ],
        out_specs=pl.BlockSpec(memory_space=pl.ANY),
        out_shape=jax.ShapeDtypeStruct((n_rows, ROW_SLAB, 128), jnp.uint32),
        scratch_shapes=[pltpu.VMEM((MOE_BLOCK, ROW_SLAB, 128), jnp.uint32),
                        pltpu.SemaphoreType.DMA(()), pltpu.SemaphoreType.DMA(())],
        compiler_params=_params(("arbitrary",)),
        name="moe_dispatch",
    )(dest, pad_lo, pad_n, n_used, xp)


def _experts_kernel(be_ref, nu_ref, xs_ref, wg_ref, wu_ref, wd_ref, ys_ref):
    b = pl.program_id(0)

    @pl.when(b < nu_ref[0])
    def _():
        xb = jnp.concatenate(_unpack_rows(xs_ref, 0, MOE_BLOCK), axis=1).astype(MXU_DTYPE)
        g = _dot(xb, wg_ref[0])
        u = _dot(xb, wu_ref[0])
        hmid = (_silu(g) * u).astype(MXU_DTYPE)
        _pack_rows(_dot(hmid, wd_ref[0]), ys_ref, 0, MOE_BLOCK)

    @pl.when(b >= nu_ref[0])
    def _():
        ys_ref[...] = jnp.zeros(ys_ref.shape, ys_ref.dtype)


def _experts(xs2d, block_e, n_used, w_gate, w_up, w_down):
    d, hid = w_gate.shape[1], w_gate.shape[2]
    blk_rows = MOE_BLOCK * ROW_SLAB
    nblk = xs2d.shape[0] // blk_rows
    blk = lambda b, be, nu: (b, 0)
    used_blk = lambda b, be, nu: (jnp.minimum(b, nu[0] - 1), 0)
    grid_spec = pltpu.PrefetchScalarGridSpec(
        num_scalar_prefetch=2,
        grid=(nblk,),
        in_specs=[
            pl.BlockSpec((blk_rows, 128), used_blk),
            pl.BlockSpec((1, d, hid), lambda b, be, nu: (be[b], 0, 0)),
            pl.BlockSpec((1, d, hid), lambda b, be, nu: (be[b], 0, 0)),
            pl.BlockSpec((1, hid, d), lambda b, be, nu: (be[b], 0, 0)),
        ],
        out_specs=pl.BlockSpec((blk_rows, 128), blk),
    )
    return pl.pallas_call(
        _experts_kernel,
        grid_spec=grid_spec,
        out_shape=jax.ShapeDtypeStruct(xs2d.shape, jnp.uint32),
        compiler_params=_params(("arbitrary",)),
        name="moe_experts",
    )(block_e, n_used, xs2d, w_gate, w_up, w_down)


def _combine_kernel(dest_ref, dnext_ref, g_ref, x_ref, wsg_ref, wsu_ref, wsd_ref, lg_ref, lb_ref, ys_ref,
                    ys2d_ref, o_ref, buf_ref, sem, *, t_tile):
    i = pl.program_id(0)
    last = pl.num_programs(0) - 1
    slot_rows = TOP_K * t_tile
    slot = lax.rem(i, 2)

    def gather(dref, s):
        base = s * slot_rows

        def issue(t, carry):
            for k in range(TOP_K):
                r = pl.multiple_of((base + k * t_tile + t) * ROW_SLAB, ROW_SLAB)
                pltpu.make_async_copy(ys_ref.at[dref[k, t]], buf_ref.at[pl.ds(r, ROW_SLAB), :],
                                      sem).start(priority=k % 2)
            return carry

        lax.fori_loop(0, t_tile, issue, 0)

    def wait(s):
        n = t_tile * ROW_SLAB
        for k in range(TOP_K):
            r = pl.multiple_of((s * slot_rows + k * t_tile) * ROW_SLAB, ROW_SLAB)
            pltpu.make_async_copy(ys2d_ref.at[pl.ds(0, n), :], buf_ref.at[pl.ds(r, n), :], sem).wait()

    @pl.when(i == 0)
    def _():
        gather(dest_ref, 0)

    wait(slot)
    gather(dnext_ref, 1 - slot)

    x = x_ref[...]
    xb = x.astype(MXU_DTYPE)
    hmid = (_silu(_dot(xb, wsg_ref[...])) * _dot(xb, wsu_ref[...])).astype(MXU_DTYPE)
    z = DN_ALPHA * x + _dot(hmid, wsd_ref[...])

    gt = jnp.transpose(g_ref[...])
    cols = None
    for k in range(TOP_K):
        tiles = _unpack_rows(buf_ref, slot * slot_rows + k * t_tile, t_tile)
        gk = gt[:, k:k + 1]
        cols = [tl * gk for tl in tiles] if cols is None else [c + tl * gk for c, tl in zip(cols, tiles)]
    z = z + jnp.concatenate(cols, axis=1)
    o_ref[...] = _layer_norm(z, lg_ref[...], lb_ref[...])

    @pl.when(i == last)
    def _():
        wait(1 - slot)


def _combine(x, dest, gates, ys, ws_gate, ws_up, ws_down, ln_g, ln_b):
    n, d = x.shape
    t = min(COMBINE_T, n)
    steps = n // t
    whole = lambda i: (0, 0)
    tile = lambda i: (0, i)
    nxt = lambda i: (0, jnp.minimum(i + 1, steps - 1))
    return pl.pallas_call(
        functools.partial(_combine_kernel, t_tile=t),
        grid=(steps,),
        in_specs=[
            pl.BlockSpec((TOP_K, t), tile, memory_space=pltpu.SMEM),
            pl.BlockSpec((TOP_K, t), nxt, memory_space=pltpu.SMEM),
            pl.BlockSpec((TOP_K, t), tile),
            pl.BlockSpec((t, d), lambda i: (i, 0)),
            pl.BlockSpec(ws_gate.shape, whole),
            pl.BlockSpec(ws_up.shape, whole),
            pl.BlockSpec(ws_down.shape, whole),
            pl.BlockSpec((1, d), whole),
            pl.BlockSpec((1, d), whole),
            pl.BlockSpec(memory_space=pl.ANY),
            pl.BlockSpec(memory_space=pl.ANY),
        ],
        out_specs=pl.BlockSpec((t, d), lambda i: (i, 0)),
        out_shape=jax.ShapeDtypeStruct((n, d), jnp.float32),
        scratch_shapes=[pltpu.VMEM((2 * TOP_K * t * ROW_SLAB, 128), jnp.uint32), pltpu.SemaphoreType.DMA(())],
        compiler_params=_params(("arbitrary",)),
        name="moe_combine",
    )(dest, dest, gates, x, ws_gate, ws_up, ws_down, ln_g, ln_b, ys, ys.reshape(-1, 128))


def _moe(x, xp, w_router_t, bias_col, eid_col, inv_perm, w_gate, w_up, w_down, ws_gate, ws_up, ws_down,
         ln_g, ln_b):
    n, d = x.shape
    gate_dense, rank_dense, counts = _route(x, w_router_t, bias_col, eid_col)
    cnt = counts[:, 0].astype(jnp.int32)[inv_perm]
    padded = (cnt + MOE_BLOCK - 1) // MOE_BLOCK * MOE_BLOCK
    pends = jnp.cumsum(padded)
    pstarts = pends - padded
    nk = n * TOP_K
    n_rows = (nk + N_EXPERTS * (MOE_BLOCK - 1) + MOE_BLOCK - 1) // MOE_BLOCK * MOE_BLOCK
    nblk = n_rows // MOE_BLOCK
    blk_start = jnp.arange(nblk, dtype=jnp.int32) * MOE_BLOCK
    block_e = jnp.minimum(jnp.sum(blk_start[:, None] >= pends[None, :], axis=1), N_EXPERTS - 1).astype(jnp.int32)
    n_used = (pends[-1:] // MOE_BLOCK).astype(jnp.int32)
    perm = jnp.argsort(inv_perm)
    pstart_col = pstarts[perm].astype(jnp.float32)[:, None]
    dest, gates = _dest(gate_dense, rank_dense, pstart_col)
    xs = _dispatch(xp, dest, (pstarts + cnt).astype(jnp.int32), (padded - cnt).astype(jnp.int32), n_used, n_rows)
    ys = _experts(xs.reshape(n_rows * ROW_SLAB, 128), block_e, n_used, w_gate, w_up, w_down)
    return _combine(x, dest, gates, ys.reshape(n_rows, ROW_SLAB, 128), ws_gate, ws_up, ws_down, ln_g, ln_b)


def kernel(x, mem, w_in_a, sinks, w_in_b, lambda_qk, subln_g, w_mem_kv, w_out, ln_g, ln_b, w_router,
           router_bias, w_gate, w_up, w_down, w_shared_gate, w_shared_up, w_shared_down):
    batch, seq, d = x.shape
    n = batch * seq
    cdt = MXU_DTYPE
    qa, kva = SWA_Q_HEADS * HEAD_DIM, SWA_KV_HEADS * HEAD_DIM
    qb, vb = DIFF_HEADS * 2 * HEAD_DIM, DIFF_HEADS * DIFF_V_DIM

    rows = np.arange(N_EXPERTS)
    perm = (rows % N_GROUPS) * PER_GROUP + rows // N_GROUPS
    inv_perm = jnp.asarray(np.argsort(perm), jnp.int32)
    eid_col = jnp.asarray(perm[:, None], jnp.int32)

    xf = x.reshape(n, d)
    kv_all = _mem_kv(mem.reshape(-1, d), w_mem_kv)
    mlen = mem.shape[1]
    kv_all = kv_all.reshape(DEPTH * batch, mlen, kv_all.shape[-1])

    for i in range(DEPTH):
        j = i // 2
        if i % 2 == 0:
            w = w_in_a[j].astype(cdt)
            ws = [w[:, :qa], w[:, qa:qa + kva], w[:, qa + kva:qa + 2 * kva], w[:, qa + 2 * kva:]]
            q, k, v, qm = _project(xf, ws, (QK_SCALE, 1.0, 1.0, QK_SCALE))
            mix = _swa_attention(q, k, v, sinks[j], batch, seq)
        else:
            w = w_in_b[j].astype(cdt)
            ws = [w[:, :qb], w[:, qb:2 * qb], w[:, 2 * qb:2 * qb + vb], w[:, 2 * qb + vb:]]
            q, k, v, qm = _project(xf, ws, (QK_SCALE * LOG2E, 1.0, 1.0, QK_SCALE))
            lam_init = 0.8 - 0.6 * math.exp(-0.3 * i)
            mix = _diff_attention(q, k, v, lambda_qk[j], subln_g[j][None, :], batch, seq, lam_init)
        kv_l = kv_all[i * batch:(i + 1) * batch]
        xf, xp = _out_proj(mix, qm, kv_l, w_out[i].astype(cdt), xf, ln_g[i, 0][None, :], ln_b[i, 0][None, :],
                           batch, seq)
        w_router_t = jnp.transpose(w_router[i])[perm]
        bias_col = router_bias[i][perm][:, None]
        xf = _moe(xf, xp, w_router_t, bias_col, eid_col, inv_perm,
                  w_gate[i].astype(cdt), w_up[i].astype(cdt), w_down[i].astype(cdt),
                  w_shared_gate[i].astype(cdt), w_shared_up[i].astype(cdt), w_shared_down[i].astype(cdt),
                  ln_g[i, 1][None, :], ln_b[i, 1][None, :])
    return xf.reshape(batch, seq, d)
```

```python
import functools
import math

import jax
import jax.numpy as jnp
import numpy as np
from jax import lax
from jax.experimental import pallas as pl
from jax.experimental.pallas import tpu as pltpu

HEAD_DIM = 64
MEM_HEADS = 4
SWA_Q_HEADS = 12
SWA_KV_HEADS = 3
SWA_GROUP = SWA_Q_HEADS // SWA_KV_HEADS
WINDOW = 128
DIFF_HEADS = 6
DIFF_V_DIM = 2 * HEAD_DIM
N_EXPERTS = 64
TOP_K = 8
N_GROUPS = 8
PER_GROUP = N_EXPERTS // N_GROUPS
TOPK_GROUPS = 4
ROUTED_SCALE = 2.5
MOE_BLOCK = 512
DEPTH = 4
DN_ALPHA = (2.0 * DEPTH) ** 0.25
LN_EPS = 1e-5
NEG = -1e30
QK_SCALE = HEAD_DIM ** -0.5

MXU_DTYPE = jnp.bfloat16

PROJ_TM = 512
DIFF_TQ = 1024
ROUTE_T = 512
DISPATCH_T = 256
COMBINE_T = 128
VMEM_LIMIT = 56 * 1024 * 1024


def _alibi_slopes(n):
    def pow2_slopes(m):
        start = 2.0 ** (-8.0 / m)
        return [start ** (i + 1) for i in range(m)]
    if (n & (n - 1)) == 0:
        return pow2_slopes(n)
    c = 2 ** int(math.floor(math.log2(n)))
    return pow2_slopes(c) + pow2_slopes(2 * c)[0::2][: n - c]


def _dot(a, b):
    return jnp.dot(a, b, preferred_element_type=jnp.float32)


def _dot_nt(a, b):
    return lax.dot_general(a, b, (((1,), (1,)), ((), ())), preferred_element_type=jnp.float32)


def _params(sem):
    return pltpu.CompilerParams(dimension_semantics=sem, vmem_limit_bytes=VMEM_LIMIT)


def _layer_norm(z, g, b):
    mu = jnp.mean(z, axis=-1, keepdims=True)
    zc = z - mu
    var = jnp.mean(zc * zc, axis=-1, keepdims=True)
    return zc * lax.rsqrt(var + LN_EPS) * g + b


def _silu(x):
    return x / (1.0 + jnp.exp(-x))


def _proj_kernel(*refs, scales):
    nw = len(scales)
    x_ref = refs[0]
    w_refs = refs[1:1 + nw]
    o_refs = refs[1 + nw:]
    xb = x_ref[...].astype(MXU_DTYPE)
    for w_ref, o_ref, sc in zip(w_refs, o_refs, scales):
        acc = _dot(xb, w_ref[...])
        if sc != 1.0:
            acc = acc * sc
        o_ref[...] = acc.astype(o_ref.dtype)


def _project(x, weights, scales):
    n, d = x.shape
    tm = min(PROJ_TM, n)
    in_specs = [pl.BlockSpec((tm, d), lambda i: (i, 0))]
    in_specs += [pl.BlockSpec(w.shape, lambda i: (0, 0)) for w in weights]
    out_specs = [pl.BlockSpec((tm, w.shape[1]), lambda i: (i, 0)) for w in weights]
    out_shape = [jax.ShapeDtypeStruct((n, w.shape[1]), MXU_DTYPE) for w in weights]
    return pl.pallas_call(
        functools.partial(_proj_kernel, scales=tuple(scales)),
        grid=(n // tm,),
        in_specs=in_specs,
        out_specs=out_specs,
        out_shape=out_shape,
        compiler_params=_params(("parallel",)),
        name="in_proj",
    )(x, *weights)


def _swa_kernel(sinks_ref, bias_ref, q_ref, kp_ref, kc_ref, vp_ref, vc_ref, o_ref):
    i = pl.program_id(1)
    q = q_ref[...]
    kp, kc, vp, vc = kp_ref[...], kc_ref[...], vp_ref[...], vc_ref[...]
    rows = SWA_GROUP * WINDOW
    head_of_row = lax.broadcasted_iota(jnp.int32, (rows, 1), 0) // WINDOW
    outs = []
    for kv in range(SWA_KV_HEADS):
        ks = slice(kv * HEAD_DIM, (kv + 1) * HEAD_DIM)
        kph, kch, vph, vch = kp[:, ks], kc[:, ks], vp[:, ks], vc[:, ks]
        h0 = kv * SWA_GROUP
        qg = jnp.concatenate([q[:, (h0 + g) * HEAD_DIM:(h0 + g + 1) * HEAD_DIM] for g in range(SWA_GROUP)],
                             axis=0)
        sink = jnp.zeros((rows, 1), jnp.float32) + sinks_ref[h0]
        for g in range(1, SWA_GROUP):
            sink = jnp.where(head_of_row == g, sinks_ref[h0 + g], sink)
        sp = jnp.where(i > 0, _dot_nt(qg, kph) + bias_ref[kv, 0], NEG)
        sc = _dot_nt(qg, kch) + bias_ref[kv, 1]
        m = jnp.maximum(jnp.maximum(jnp.max(sp, axis=-1, keepdims=True),
                                    jnp.max(sc, axis=-1, keepdims=True)), sink)
        pp = jnp.exp(sp - m)
        pc = jnp.exp(sc - m)
        denom = (jnp.sum(pp, axis=-1, keepdims=True) + jnp.sum(pc, axis=-1, keepdims=True)
                 + jnp.exp(sink - m))
        o = (_dot(pp.astype(MXU_DTYPE), vph) + _dot(pc.astype(MXU_DTYPE), vch)) / denom
        outs += [o[g * WINDOW:(g + 1) * WINDOW] for g in range(SWA_GROUP)]
    o_ref[...] = jnp.concatenate(outs, axis=-1).astype(o_ref.dtype)


def _swa_bias():
    slopes = np.asarray(_alibi_slopes(SWA_Q_HEADS), np.float64).reshape(SWA_KV_HEADS, SWA_GROUP)
    qi = np.arange(WINDOW)[:, None]
    kj = np.arange(WINDOW)[None, :]
    dist_c = qi - kj
    dist_p = dist_c + WINDOW
    out = np.empty((SWA_KV_HEADS, 2, SWA_GROUP, WINDOW, WINDOW), np.float32)
    for kv in range(SWA_KV_HEADS):
        for g in range(SWA_GROUP):
            out[kv, 0, g] = np.where(dist_p < WINDOW, -slopes[kv, g] * dist_p, NEG)
            out[kv, 1, g] = np.where(dist_c >= 0, -slopes[kv, g] * dist_c, NEG)
    return jnp.asarray(out.reshape(SWA_KV_HEADS, 2, SWA_GROUP * WINDOW, WINDOW))


def _swa_attention(q, k, v, sinks, batch, seq):
    n = batch * seq
    nb = seq // WINDOW
    qw, kw = q.shape[1], k.shape[1]
    cur = lambda b, i: (b * nb + i, 0)
    prev = lambda b, i: (b * nb + jnp.maximum(i - 1, 0), 0)
    bias = _swa_bias()
    return pl.pallas_call(
        _swa_kernel,
        grid=(batch, nb),
        in_specs=[
            pl.BlockSpec(memory_space=pltpu.SMEM),
            pl.BlockSpec(bias.shape, lambda b, i: (0, 0, 0, 0)),
            pl.BlockSpec((WINDOW, qw), cur),
            pl.BlockSpec((WINDOW, kw), prev),
            pl.BlockSpec((WINDOW, kw), cur),
            pl.BlockSpec((WINDOW, kw), prev),
            pl.BlockSpec((WINDOW, kw), cur),
        ],
        out_specs=pl.BlockSpec((WINDOW, qw), cur),
        out_shape=jax.ShapeDtypeStruct((n, qw), MXU_DTYPE),
        compiler_params=_params(("parallel", "parallel")),
        name="swa_attention",
    )(sinks, bias, q, k, k, v, v)


LOG2E = math.log2(math.e)
_BIAS_PIECES = 3


def _bf16_pieces(c):
    out, rest = [], np.float64(c)
    for _ in range(_BIAS_PIECES):
        piece = np.asarray(rest, np.float32).astype(jnp.bfloat16).astype(np.float64)
        out.append(float(piece))
        rest = rest - piece
    return out


def _alibi_templates(tq):
    idx = np.arange(tq)
    hi, lo = (idx // 256) * 256, idx % 256
    hw = 2 * HEAD_DIM
    qt = np.zeros((DIFF_HEADS, 2, tq, hw), np.float32)
    kt = np.zeros((DIFF_HEADS, 2, tq, hw), np.float32)
    for h, slope in enumerate(_alibi_slopes(DIFF_HEADS)):
        for c in range(2):
            base = HEAD_DIM if c == 0 else 0
            for p, piece in enumerate(_bf16_pieces(slope * LOG2E)):
                qt[h, c, :, base + p] = -hi
                kt[h, c, :, base + p] = piece
                qt[h, c, :, base + _BIAS_PIECES + p] = -lo
                kt[h, c, :, base + _BIAS_PIECES + p] = piece
                qt[h, c, :, base + 2 * _BIAS_PIECES + p] = piece
                kt[h, c, :, base + 2 * _BIAS_PIECES + p] = hi
                qt[h, c, :, base + 3 * _BIAS_PIECES + p] = piece
                kt[h, c, :, base + 3 * _BIAS_PIECES + p] = lo
    return jnp.asarray(qt, jnp.bfloat16), jnp.asarray(kt, jnp.bfloat16)


def _diff_kernel(slopes_ref, q_ref, k_ref, v_ref, qt_ref, kt_ref, ones_ref, dmask_ref, lam_ref, g_ref,
                 o_ref, qa_ref, m_ref, acc_ref, *, tq, lam_init):
    h = pl.program_id(1)
    qi = pl.program_id(2)
    slope2 = slopes_ref[h]
    hw = 2 * HEAD_DIM
    lane = lax.broadcasted_iota(jnp.int32, (1, hw), 1)
    first = lane < HEAD_DIM
    q = q_ref[...]
    qa_ref[0] = jnp.where(first, q, qt_ref[0, 0])
    qa_ref[1] = jnp.where(first, qt_ref[0, 1], q)
    m_ref[...] = jnp.full(m_ref.shape, NEG, jnp.float32)
    acc_ref[...] = jnp.zeros(acc_ref.shape, jnp.float32)
    ones = ones_ref[...]

    def softmax_pv(c, u, vaug, shift):
        m_old = m_ref[c]
        m_new = jnp.maximum(m_old, jnp.max(u, axis=-1, keepdims=True) + shift)
        p = jnp.exp2(u - (m_new - shift)).astype(MXU_DTYPE)
        alpha = jnp.exp2(m_old - m_new)
        acc_ref[c] = alpha * acc_ref[c] + _dot(p, vaug)
        m_ref[c] = m_new

    def process(ki, shift, diagonal):
        start = pl.multiple_of(ki * tq, tq)
        k = k_ref[pl.ds(start, tq), :]
        v = v_ref[pl.ds(start, tq), :]
        vaug = jnp.concatenate([v, ones], axis=1)
        kas = (jnp.where(first, k, kt_ref[0, 0]), jnp.where(first, kt_ref[0, 1], k))
        us = [_dot_nt(qa_ref[c], kas[c]) for c in range(2)]
        if diagonal:
            us = [u + dmask_ref[...] for u in us]
        for c in range(2):
            softmax_pv(c, us[c], vaug, shift)

    def off_diag(ki, carry):
        process(ki, slope2 * ((ki - qi) * tq).astype(jnp.float32), False)
        return carry

    lax.fori_loop(0, qi, off_diag, 0)
    process(qi, jnp.float32(0.0), True)

    lp = lam_ref[...]
    lam = (jnp.exp(jnp.sum(lp[0:1] * lp[1:2], axis=-1, keepdims=True))
           - jnp.exp(jnp.sum(lp[2:3] * lp[3:4], axis=-1, keepdims=True)) + lam_init)
    a0, a1 = acc_ref[0], acc_ref[1]
    o = a0[:, :hw] / a0[:, hw:hw + 1] - lam * (a1[:, :hw] / a1[:, hw:hw + 1])
    ms = jnp.mean(o * o, axis=-1, keepdims=True)
    o = o * lax.rsqrt(ms + LN_EPS) * g_ref[...] * (1.0 - lam_init)
    o_ref[...] = o.astype(o_ref.dtype)


def _diff_attention(q, k, v, lam_qk, subln_g, batch, seq, lam_init):
    n = batch * seq
    tq = min(DIFF_TQ, seq)
    nq = seq // tq
    hw = 2 * HEAD_DIM
    slopes2 = np.asarray(_alibi_slopes(DIFF_HEADS), np.float64) * LOG2E
    qt, kt = _alibi_templates(tq)
    ones_col = np.zeros((tq, hw), np.float32)
    ones_col[:, 0] = 1.0
    rel = np.arange(tq)[None, :] - np.arange(tq)[:, None]
    dmask = jnp.asarray(np.where(rel <= 0, 0.0, NEG).astype(np.float32))
    kernel = functools.partial(_diff_kernel, tq=tq, lam_init=lam_init)
    fixed = lambda b, h, i: (0, 0)
    per_head = lambda b, h, i: (h, 0, 0, 0)
    return pl.pallas_call(
        kernel,
        grid=(batch, DIFF_HEADS, nq),
        in_specs=[
            pl.BlockSpec(memory_space=pltpu.SMEM),
            pl.BlockSpec((tq, hw), lambda b, h, i: (b * nq + i, h)),
            pl.BlockSpec((seq, hw), lambda b, h, i: (b, h)),
            pl.BlockSpec((seq, hw), lambda b, h, i: (b, h)),
            pl.BlockSpec((1, 2, tq, hw), per_head),
            pl.BlockSpec((1, 2, tq, hw), per_head),
            pl.BlockSpec((tq, hw), fixed),
            pl.BlockSpec((tq, tq), fixed),
            pl.BlockSpec(lam_qk.shape, fixed),
            pl.BlockSpec(subln_g.shape, fixed),
        ],
        out_specs=pl.BlockSpec((tq, hw), lambda b, h, i: (b * nq + i, h)),
        out_shape=jax.ShapeDtypeStruct((n, DIFF_HEADS * hw), MXU_DTYPE),
        scratch_shapes=[
            pltpu.VMEM((2, tq, hw), MXU_DTYPE),
            pltpu.VMEM((2, tq, 1), jnp.float32),
            pltpu.VMEM((2, tq, 2 * hw), jnp.float32),
        ],
        compiler_params=_params(("parallel", "parallel", "arbitrary")),
        name="diff_attention",
    )(jnp.asarray(slopes2, jnp.float32), q, k, v, qt, kt, jnp.asarray(ones_col, MXU_DTYPE), dmask,
      lam_qk, subln_g)


def _memkv_kernel(mem_ref, w_ref, o_ref):
    o_ref[0] = _dot(mem_ref[...].astype(MXU_DTYPE), w_ref[0].astype(MXU_DTYPE)).astype(o_ref.dtype)


def _mem_kv(mem2d, w_mem_kv):
    depth, d, w = w_mem_kv.shape
    rows = mem2d.shape[0]
    return pl.pallas_call(
        _memkv_kernel,
        grid=(depth,),
        in_specs=[pl.BlockSpec((rows, d), lambda l: (0, 0)),
                  pl.BlockSpec((1, d, w), lambda l: (l, 0, 0))],
        out_specs=pl.BlockSpec((1, rows, w), lambda l: (l, 0, 0)),
        out_shape=jax.ShapeDtypeStruct((depth, rows, w), MXU_DTYPE),
        compiler_params=_params(("parallel",)),
        name="mem_kv_proj",
    )(mem2d, w_mem_kv)


ROW_SLAB = 4
_HI16 = 0xFFFF0000


def _bf16_bits_hi(x):
    b = lax.bitcast_convert_type(x, jnp.uint32)
    b = b + jnp.uint32(0x7FFF) + ((b >> 16) & jnp.uint32(1))
    return b & jnp.uint32(_HI16)


def _pack_rows(z, ref, base, t):
    half = z.shape[1] // 2
    words = _bf16_bits_hi(z[:, half:]) | (_bf16_bits_hi(z[:, :half]) >> 16)
    for s in range(ROW_SLAB):
        ref[pl.ds(base * ROW_SLAB + s, t, stride=ROW_SLAB), :] = words[:, s * 128:(s + 1) * 128]


def _unpack_rows(ref, base, t):
    lo, hi = [], []
    for s in range(ROW_SLAB):
        w = ref[pl.ds(base * ROW_SLAB + s, t, stride=ROW_SLAB), :]
        lo.append(lax.bitcast_convert_type(w << 16, jnp.float32))
        hi.append(lax.bitcast_convert_type(w & jnp.uint32(_HI16), jnp.float32))
    return lo + hi


def _outproj_kernel(mix_ref, qm_ref, kv_ref, wo_ref, x_ref, g_ref, b_ref, o_ref, op_ref):
    qm = qm_ref[...]
    kvm = kv_ref[0]
    mw = MEM_HEADS * HEAD_DIM
    outs = []
    for h in range(MEM_HEADS):
        hs = slice(h * HEAD_DIM, (h + 1) * HEAD_DIM)
        s = _dot_nt(qm[:, hs], kvm[:, hs])
        m = jnp.max(s, axis=-1, keepdims=True)
        p = jnp.exp(s - m)
        l = jnp.sum(p, axis=-1, keepdims=True)
        vh = kvm[:, mw + h * HEAD_DIM: mw + (h + 1) * HEAD_DIM]
        outs.append(_dot(p.astype(MXU_DTYPE), vh) / l)
    mem_out = jnp.concatenate(outs, axis=-1).astype(MXU_DTYPE)
    mixw = mix_ref.shape[1]
    y = _dot(mix_ref[...], wo_ref[0:mixw, :]) + _dot(mem_out, wo_ref[mixw:, :])
    z = DN_ALPHA * x_ref[...] + y
    zn = _layer_norm(z, g_ref[...], b_ref[...])
    o_ref[...] = zn
    _pack_rows(zn, op_ref, 0, zn.shape[0])


def _out_proj(mix, qm, kv_l, w_out, x, ln_g, ln_b, batch, seq):
    n, d = x.shape
    tm = min(PROJ_TM, seq)
    per_b = seq // tm
    mlen, kvw = kv_l.shape[1], kv_l.shape[2]
    row = lambda i: (i, 0)
    whole = lambda i: (0, 0)
    xn, xp = pl.pallas_call(
        _outproj_kernel,
        grid=(n // tm,),
        in_specs=[
            pl.BlockSpec((tm, mix.shape[1]), row),
            pl.BlockSpec((tm, qm.shape[1]), row),
            pl.BlockSpec((1, mlen, kvw), lambda i: (i // per_b, 0, 0)),
            pl.BlockSpec(w_out.shape, whole),
            pl.BlockSpec((tm, d), row),
            pl.BlockSpec((1, d), whole),
            pl.BlockSpec((1, d), whole),
        ],
        out_specs=[pl.BlockSpec((tm, d), row), pl.BlockSpec((tm * ROW_SLAB, 128), row)],
        out_shape=[jax.ShapeDtypeStruct((n, d), jnp.float32),
                   jax.ShapeDtypeStruct((n * ROW_SLAB, 128), jnp.uint32)],
        compiler_params=_params(("parallel",)),
        name="out_proj_ln",
    )(mix, qm, kv_l, w_out, x, ln_g, ln_b)
    return xn, xp.reshape(n, ROW_SLAB, 128)


def _split_hi_lo(a):
    hi = a.astype(jnp.bfloat16)
    lo = (a - hi.astype(jnp.float32)).astype(jnp.bfloat16)
    return hi, lo


def _route_kernel(x_ref, w_ref, bias_ref, eid_ref, tri_ref, gate_ref, rank_ref, cnt_ref, carry_ref):
    i = pl.program_id(0)

    @pl.when(i == 0)
    def _():
        carry_ref[...] = jnp.zeros(carry_ref.shape, jnp.float32)

    xh, xl = _split_hi_lo(x_ref[...])
    wh, wl = _split_hi_lo(w_ref[...])
    logits = _dot_nt(wh, xh) + _dot_nt(wh, xl) + _dot_nt(wl, xh)
    aff = 1.0 / (1.0 + jnp.exp(-logits))
    sel = aff + bias_ref[...]
    t = sel.shape[1]

    sel3 = sel.reshape(PER_GROUP, N_GROUPS, t)
    jidx = lax.broadcasted_iota(jnp.int32, (PER_GROUP, N_GROUPS, t), 0)
    m1 = jnp.max(sel3, axis=0, keepdims=True)
    first = jnp.min(jnp.where(sel3 == m1, jidx, PER_GROUP), axis=0, keepdims=True)
    m2 = jnp.max(jnp.where(jidx == first, -jnp.inf, sel3), axis=0, keepdims=True)
    gscore = (m1 + m2)[0]

    gid = lax.broadcasted_iota(jnp.int32, (N_GROUPS, t), 0)
    grank = jnp.zeros((N_GROUPS, t), jnp.int32)
    for g2 in range(N_GROUPS):
        other = gscore[g2:g2 + 1, :]
        beats = jnp.logical_or(other > gscore, jnp.logical_and(other == gscore, g2 < gid))
        grank = grank + beats.astype(jnp.int32)
    gmask = grank < TOPK_GROUPS
    emask = jnp.broadcast_to(gmask[None], (PER_GROUP, N_GROUPS, t)).reshape(N_EXPERTS, t)
    val = jnp.where(emask, sel, -jnp.inf)

    eid = eid_ref[...]
    erank = jnp.zeros((N_EXPERTS, t), jnp.int32)
    for r2 in range(N_EXPERTS):
        other = val[r2:r2 + 1, :]
        e2 = (r2 % N_GROUPS) * PER_GROUP + r2 // N_GROUPS
        beats = jnp.logical_or(other > val, jnp.logical_and(other == val, e2 < eid))
        erank = erank + beats.astype(jnp.int32)
    chosen = jnp.logical_and(erank < TOP_K, emask)
    picked = jnp.where(chosen, aff, 0.0)
    gate_ref[...] = picked / jnp.sum(picked, axis=0, keepdims=True) * ROUTED_SCALE

    ind = chosen.astype(jnp.bfloat16)
    prefix = _dot(ind, tri_ref[...])
    carry = carry_ref[...]
    rank_ref[...] = jnp.where(chosen, prefix + carry, -1.0)
    carry = carry + jnp.sum(chosen.astype(jnp.float32), axis=1, keepdims=True)
    carry_ref[...] = carry
    cnt_ref[...] = carry


def _route(x, w_router_t, bias_col, eid_col):
    n, d = x.shape
    t = min(ROUTE_T, n)
    tri = jnp.asarray(np.triu(np.ones((t, t), np.float32), 1), jnp.bfloat16)
    whole = lambda i: (0, 0)
    return pl.pallas_call(
        _route_kernel,
        grid=(n // t,),
        in_specs=[
            pl.BlockSpec((t, d), lambda i: (i, 0)),
            pl.BlockSpec((N_EXPERTS, d), whole),
            pl.BlockSpec((N_EXPERTS, 1), whole),
            pl.BlockSpec((N_EXPERTS, 1), whole),
            pl.BlockSpec((t, t), whole),
        ],
        out_specs=[
            pl.BlockSpec((N_EXPERTS, t), lambda i: (0, i)),
            pl.BlockSpec((N_EXPERTS, t), lambda i: (0, i)),
            pl.BlockSpec((N_EXPERTS, 1), whole),
        ],
        out_shape=[
            jax.ShapeDtypeStruct((N_EXPERTS, n), jnp.float32),
            jax.ShapeDtypeStruct((N_EXPERTS, n), jnp.float32),
            jax.ShapeDtypeStruct((N_EXPERTS, 1), jnp.float32),
        ],
        scratch_shapes=[pltpu.VMEM((N_EXPERTS, 1), jnp.float32)],
        compiler_params=_params(("arbitrary",)),
        name="moe_route",
    )(x, w_router_t, bias_col, eid_col, tri)


def _dest_kernel(gate_ref, rank_ref, pstart_ref, ltri_ref, dest_ref, g_ref):
    rank = rank_ref[...]
    chosen = rank >= 0.0
    slot = _dot(ltri_ref[...], chosen.astype(jnp.bfloat16))
    row = rank + pstart_ref[...]
    gate = gate_ref[...]
    dests, gates = [], []
    for k in range(TOP_K):
        hit = jnp.logical_and(chosen, slot == float(k))
        dests.append(jnp.sum(jnp.where(hit, row, 0.0), axis=0, keepdims=True))
        gates.append(jnp.sum(jnp.where(hit, gate, 0.0), axis=0, keepdims=True))
    dest_ref[...] = jnp.concatenate(dests, axis=0).astype(jnp.int32)
    g_ref[...] = jnp.concatenate(gates, axis=0)


def _dest(gate_dense, rank_dense, pstart_col):
    e, n = gate_dense.shape
    t = min(ROUTE_T, n)
    ltri = jnp.asarray(np.tril(np.ones((e, e), np.float32), -1), jnp.bfloat16)
    whole = lambda i: (0, 0)
    tile = lambda i: (0, i)
    return pl.pallas_call(
        _dest_kernel,
        grid=(n // t,),
        in_specs=[pl.BlockSpec((e, t), tile), pl.BlockSpec((e, t), tile),
                  pl.BlockSpec((e, 1), whole), pl.BlockSpec((e, e), whole)],
        out_specs=[pl.BlockSpec((TOP_K, t), tile), pl.BlockSpec((TOP_K, t), tile)],
        out_shape=[jax.ShapeDtypeStruct((TOP_K, n), jnp.int32),
                   jax.ShapeDtypeStruct((TOP_K, n), jnp.float32)],
        compiler_params=_params(("parallel",)),
        name="moe_dest",
    )(gate_dense, rank_dense, pstart_col, ltri)


def _dispatch_kernel(dest_ref, padlo_ref, padn_ref, nu_ref, xp_ref, xs_ref, zero_ref, sem, zsem, *,
                     t_tile, nblk):
    i = pl.program_id(0)

    @pl.when(i == 0)
    def _():
        zero_ref[...] = jnp.zeros(zero_ref.shape, zero_ref.dtype)

        def for_each_pad_copy(e, fn):
            off = padlo_ref[e]
            cnt = padn_ref[e]
            for bit in reversed(range(MOE_BLOCK.bit_length() - 1)):
                size = 1 << bit
                on = (cnt & size) != 0

                @pl.when(on)
                def _():
                    fn(pltpu.make_async_copy(zero_ref.at[pl.ds(0, size)], xs_ref.at[pl.ds(off, size)], zsem))
                off = off + jnp.where(on, size, 0)

        def tail_copy(b):
            return pltpu.make_async_copy(zero_ref, xs_ref.at[pl.ds(b * MOE_BLOCK, MOE_BLOCK)], zsem)

        def start_pad(e, c):
            for_each_pad_copy(e, lambda cp: cp.start())
            return c

        def wait_pad(e, c):
            for_each_pad_copy(e, lambda cp: cp.wait())
            return c

        def start_tail(b, c):
            tail_copy(b).start()
            return c

        def wait_tail(b, c):
            tail_copy(b).wait()
            return c

        lax.fori_loop(0, N_EXPERTS, start_pad, 0)
        lax.fori_loop(nu_ref[0], nblk, start_tail, 0)
        lax.fori_loop(0, N_EXPERTS, wait_pad, 0)
        lax.fori_loop(nu_ref[0], nblk, wait_tail, 0)

    def issue(t, carry):
        for k in range(TOP_K):
            d = dest_ref[k, t]
            pltpu.make_async_copy(xp_ref.at[t], xs_ref.at[d], sem).start(priority=k % 2)
        return carry

    lax.fori_loop(0, t_tile, issue, 0)
    for k in range(TOP_K):
        pltpu.make_async_copy(xp_ref, xs_ref.at[pl.ds(0, t_tile)], sem).wait()


def _dispatch(xp, dest, pad_lo, pad_n, n_used, n_rows):
    n = xp.shape[0]
    t = min(DISPATCH_T, n)
    nblk = n_rows // MOE_BLOCK
    smem = pl.BlockSpec(memory_space=pltpu.SMEM)
    return pl.pallas_call(
        functools.partial(_dispatch_kernel, t_tile=t, nblk=nblk),
        grid=(n // t,),
        in_specs=[
            pl.BlockSpec((TOP_K, t), lambda i: (0, i), memory_space=pltpu.SMEM),
            smem, smem, smem,
            pl.BlockSpec((t, ROW_SLAB, 128), lambda i: (i, 0, 0)),
        ],
        out_specs=pl.BlockSpec(memory_space=pl.ANY),
        out_shape=jax.ShapeDtypeStruct((n_rows, ROW_SLAB, 128), jnp.uint32),
        scratch_shapes=[pltpu.VMEM((MOE_BLOCK, ROW_SLAB, 128), jnp.uint32),
                        pltpu.SemaphoreType.DMA(()), pltpu.SemaphoreType.DMA(())],
        compiler_params=_params(("arbitrary",)),
        name="moe_dispatch",
    )(dest, pad_lo, pad_n, n_used, xp)


def _experts_kernel(be_ref, nu_ref, xs_ref, wg_ref, wu_ref, wd_ref, ys_ref, wgc_ref, wuc_ref, wdc_ref):
    b = pl.program_id(0)
    new_expert = jnp.logical_or(b == 0, be_ref[b] != be_ref[jnp.maximum(b - 1, 0)])

    @pl.when(jnp.logical_and(b < nu_ref[0], new_expert))
    def _():
        wgc_ref[...] = wg_ref[0, 0].astype(MXU_DTYPE)
        wuc_ref[...] = wu_ref[0, 0].astype(MXU_DTYPE)
        wdc_ref[...] = wd_ref[0, 0].astype(MXU_DTYPE)

    @pl.when(b < nu_ref[0])
    def _():
        xb = jnp.concatenate(_unpack_rows(xs_ref, 0, MOE_BLOCK), axis=1).astype(MXU_DTYPE)
        g = _dot(xb, wgc_ref[...])
        u = _dot(xb, wuc_ref[...])
        hmid = (_silu(g) * u).astype(MXU_DTYPE)
        _pack_rows(_dot(hmid, wdc_ref[...]), ys_ref, 0, MOE_BLOCK)

    @pl.when(b >= nu_ref[0])
    def _():
        ys_ref[...] = jnp.zeros(ys_ref.shape, ys_ref.dtype)


def _experts(xs2d, block_e, n_used, w_gate, w_up, w_down, layer):
    d, hid = w_gate.shape[2], w_gate.shape[3]
    blk_rows = MOE_BLOCK * ROW_SLAB
    nblk = xs2d.shape[0] // blk_rows
    blk = lambda b, be, nu: (b, 0)
    used_blk = lambda b, be, nu: (jnp.minimum(b, nu[0] - 1), 0)
    expert = lambda b, be, nu: (layer, be[b], 0, 0)
    grid_spec = pltpu.PrefetchScalarGridSpec(
        num_scalar_prefetch=2,
        grid=(nblk,),
        in_specs=[
            pl.BlockSpec((blk_rows, 128), used_blk),
            pl.BlockSpec((1, 1, d, hid), expert),
            pl.BlockSpec((1, 1, d, hid), expert),
            pl.BlockSpec((1, 1, hid, d), expert),
        ],
        out_specs=pl.BlockSpec((blk_rows, 128), blk),
        scratch_shapes=[pltpu.VMEM((d, hid), MXU_DTYPE), pltpu.VMEM((d, hid), MXU_DTYPE),
                        pltpu.VMEM((hid, d), MXU_DTYPE)],
    )
    return pl.pallas_call(
        _experts_kernel,
        grid_spec=grid_spec,
        out_shape=jax.ShapeDtypeStruct(xs2d.shape, jnp.uint32),
        compiler_params=_params(("arbitrary",)),
        name="moe_experts",
    )(block_e, n_used, xs2d, w_gate, w_up, w_down)


def _combine_kernel(dest_ref, dnext_ref, g_ref, x_ref, wsg_ref, wsu_ref, wsd_ref, lg_ref, lb_ref, ys_ref,
                    ys2d_ref, o_ref, buf_ref, sem, *, t_tile):
    i = pl.program_id(0)
    last = pl.num_programs(0) - 1
    slot_rows = TOP_K * t_tile
    slot = lax.rem(i, 2)

    def gather(dref, s, unrolled):
        base = s * slot_rows

        def issue(t, carry):
            for k in range(TOP_K):
                r = pl.multiple_of((base + k * t_tile + t) * ROW_SLAB, ROW_SLAB)
                pltpu.make_async_copy(ys_ref.at[dref[k, t]], buf_ref.at[pl.ds(r, ROW_SLAB), :],
                                      sem).start(priority=k % 2)
            return carry

        if unrolled:
            for t in range(t_tile):
                issue(t, 0)
        else:
            lax.fori_loop(0, t_tile, issue, 0)

    def wait(s):
        n = t_tile * ROW_SLAB
        for k in range(TOP_K):
            r = pl.multiple_of((s * slot_rows + k * t_tile) * ROW_SLAB, ROW_SLAB)
            pltpu.make_async_copy(ys2d_ref.at[pl.ds(0, n), :], buf_ref.at[pl.ds(r, n), :], sem).wait()

    @pl.when(i == 0)
    def _():
        gather(dest_ref, 0, False)

    wait(slot)
    gather(dnext_ref, 1 - slot, True)

    x = x_ref[...]
    xb = x.astype(MXU_DTYPE)
    hmid = (_silu(_dot(xb, wsg_ref[...])) * _dot(xb, wsu_ref[...])).astype(MXU_DTYPE)
    z = DN_ALPHA * x + _dot(hmid, wsd_ref[...])

    gt = jnp.transpose(g_ref[...])
    cols = None
    for k in range(TOP_K):
        tiles = _unpack_rows(buf_ref, slot * slot_rows + k * t_tile, t_tile)
        gk = gt[:, k:k + 1]
        cols = [tl * gk for tl in tiles] if cols is None else [c + tl * gk for c, tl in zip(cols, tiles)]
    z = z + jnp.concatenate(cols, axis=1)
    o_ref[...] = _layer_norm(z, lg_ref[...], lb_ref[...])

    @pl.when(i == last)
    def _():
        wait(1 - slot)


def _combine(x, dest, gates, ys, ws_gate, ws_up, ws_down, ln_g, ln_b):
    n, d = x.shape
    t = min(COMBINE_T, n)
    steps = n // t
    whole = lambda i: (0, 0)
    tile = lambda i: (0, i)
    nxt = lambda i: (0, jnp.minimum(i + 1, steps - 1))
    return pl.pallas_call(
        functools.partial(_combine_kernel, t_tile=t),
        grid=(steps,),
        in_specs=[
            pl.BlockSpec((TOP_K, t), tile, memory_space=pltpu.SMEM),
            pl.BlockSpec((TOP_K, t), nxt, memory_space=pltpu.SMEM),
            pl.BlockSpec((TOP_K, t), tile),
            pl.BlockSpec((t, d), lambda i: (i, 0)),
            pl.BlockSpec(ws_gate.shape, whole),
            pl.BlockSpec(ws_up.shape, whole),
            pl.BlockSpec(ws_down.shape, whole),
            pl.BlockSpec((1, d), whole),
            pl.BlockSpec((1, d), whole),
            pl.BlockSpec(memory_space=pl.ANY),
            pl.BlockSpec(memory_space=pl.ANY),
        ],
        out_specs=pl.BlockSpec((t, d), lambda i: (i, 0)),
        out_shape=jax.ShapeDtypeStruct((n, d), jnp.float32),
        scratch_shapes=[pltpu.VMEM((2 * TOP_K * t * ROW_SLAB, 128), jnp.uint32), pltpu.SemaphoreType.DMA(())],
        compiler_params=_params(("arbitrary",)),
        name="moe_combine",
    )(dest, dest, gates, x, ws_gate, ws_up, ws_down, ln_g, ln_b, ys, ys.reshape(-1, 128))


def _moe(x, xp, w_router_t, bias_col, eid_col, inv_perm, w_gate, w_up, w_down, layer, ws_gate, ws_up, ws_down,
         ln_g, ln_b):
    n, d = x.shape
    gate_dense, rank_dense, counts = _route(x, w_router_t, bias_col, eid_col)
    cnt = counts[:, 0].astype(jnp.int32)[inv_perm]
    padded = (cnt + MOE_BLOCK - 1) // MOE_BLOCK * MOE_BLOCK
    pends = jnp.cumsum(padded)
    pstarts = pends - padded
    nk = n * TOP_K
    n_rows = (nk + N_EXPERTS * (MOE_BLOCK - 1) + MOE_BLOCK - 1) // MOE_BLOCK * MOE_BLOCK
    nblk = n_rows // MOE_BLOCK
    blk_start = jnp.arange(nblk, dtype=jnp.int32) * MOE_BLOCK
    block_e = jnp.minimum(jnp.sum(blk_start[:, None] >= pends[None, :], axis=1), N_EXPERTS - 1).astype(jnp.int32)
    n_used = (pends[-1:] // MOE_BLOCK).astype(jnp.int32)
    perm = jnp.argsort(inv_perm)
    pstart_col = pstarts[perm].astype(jnp.float32)[:, None]
    dest, gates = _dest(gate_dense, rank_dense, pstart_col)
    xs = _dispatch(xp, dest, (pstarts + cnt).astype(jnp.int32), (padded - cnt).astype(jnp.int32), n_used, n_rows)
    ys = _experts(xs.reshape(n_rows * ROW_SLAB, 128), block_e, n_used, w_gate, w_up, w_down, layer)
    return _combine(x, dest, gates, ys.reshape(n_rows, ROW_SLAB, 128), ws_gate, ws_up, ws_down, ln_g, ln_b)


def kernel(x, mem, w_in_a, sinks, w_in_b, lambda_qk, subln_g, w_mem_kv, w_out, ln_g, ln_b, w_router,
           router_bias, w_gate, w_up, w_down, w_shared_gate, w_shared_up, w_shared_down):
    batch, seq, d = x.shape
    n = batch * seq
    cdt = MXU_DTYPE
    qa, kva = SWA_Q_HEADS * HEAD_DIM, SWA_KV_HEADS * HEAD_DIM
    qb, vb = DIFF_HEADS * 2 * HEAD_DIM, DIFF_HEADS * DIFF_V_DIM

    rows = np.arange(N_EXPERTS)
    perm = (rows % N_GROUPS) * PER_GROUP + rows // N_GROUPS
    inv_perm = jnp.asarray(np.argsort(perm), jnp.int32)
    eid_col = jnp.asarray(perm[:, None], jnp.int32)

    xf = x.reshape(n, d)
    kv_all = _mem_kv(mem.reshape(-1, d), w_mem_kv)
    mlen = mem.shape[1]
    kv_all = kv_all.reshape(DEPTH * batch, mlen, kv_all.shape[-1])

    for i in range(DEPTH):
        j = i // 2
        if i % 2 == 0:
            w = w_in_a[j].astype(cdt)
            ws = [w[:, :qa], w[:, qa:qa + kva], w[:, qa + kva:qa + 2 * kva], w[:, qa + 2 * kva:]]
            q, k, v, qm = _project(xf, ws, (QK_SCALE, 1.0, 1.0, QK_SCALE))
            mix = _swa_attention(q, k, v, sinks[j], batch, seq)
        else:
            w = w_in_b[j].astype(cdt)
            ws = [w[:, :qb], w[:, qb:2 * qb], w[:, 2 * qb:2 * qb + vb], w[:, 2 * qb + vb:]]
            q, k, v, qm = _project(xf, ws, (QK_SCALE * LOG2E, 1.0, 1.0, QK_SCALE))
            lam_init = 0.8 - 0.6 * math.exp(-0.3 * i)
            mix = _diff_attention(q, k, v, lambda_qk[j], subln_g[j][None, :], batch, seq, lam_init)
        kv_l = kv_all[i * batch:(i + 1) * batch]
        xf, xp = _out_proj(mix, qm, kv_l, w_out[i].astype(cdt), xf, ln_g[i, 0][None, :], ln_b[i, 0][None, :],
                           batch, seq)
        w_router_t = jnp.transpose(w_router[i])[perm]
        bias_col = router_bias[i][perm][:, None]
        xf = _moe(xf, xp, w_router_t, bias_col, eid_col, inv_perm,
                  w_gate, w_up, w_down, i,
                  w_shared_gate[i].astype(cdt), w_shared_up[i].astype(cdt), w_shared_down[i].astype(cdt),
                  ln_g[i, 1][None, :], ln_b[i, 1][None, :])
    return xf.reshape(batch, seq, d)
```

```python
import functools
import math

import jax
import jax.numpy as jnp
import numpy as np
from jax import lax
from jax.experimental import pallas as pl
from jax.experimental.pallas import tpu as pltpu

HEAD_DIM = 64
MEM_HEADS = 4
SWA_Q_HEADS = 12
SWA_KV_HEADS = 3
SWA_GROUP = SWA_Q_HEADS // SWA_KV_HEADS
WINDOW = 128
DIFF_HEADS = 6
DIFF_V_DIM = 2 * HEAD_DIM
N_EXPERTS = 64
TOP_K = 8
N_GROUPS = 8
PER_GROUP = N_EXPERTS // N_GROUPS
TOPK_GROUPS = 4
ROUTED_SCALE = 2.5
MOE_BLOCK = 512
DEPTH = 4
DN_ALPHA = (2.0 * DEPTH) ** 0.25
LN_EPS = 1e-5
NEG = -1e30
QK_SCALE = HEAD_DIM ** -0.5

MXU_DTYPE = jnp.bfloat16

PROJ_TM = 512
DIFF_TQ = 1024
ROUTE_T = 512
DISPATCH_T = 256
COMBINE_T = 128
VMEM_LIMIT = 56 * 1024 * 1024


def _alibi_slopes(n):
    def pow2_slopes(m):
        start = 2.0 ** (-8.0 / m)
        return [start ** (i + 1) for i in range(m)]
    if (n & (n - 1)) == 0:
        return pow2_slopes(n)
    c = 2 ** int(math.floor(math.log2(n)))
    return pow2_slopes(c) + pow2_slopes(2 * c)[0::2][: n - c]


def _dot(a, b):
    return jnp.dot(a, b, preferred_element_type=jnp.float32)


def _dot_nt(a, b):
    return lax.dot_general(a, b, (((1,), (1,)), ((), ())), preferred_element_type=jnp.float32)


def _params(sem):
    return pltpu.CompilerParams(dimension_semantics=sem, vmem_limit_bytes=VMEM_LIMIT)


def _layer_norm(z, g, b):
    mu = jnp.mean(z, axis=-1, keepdims=True)
    zc = z - mu
    var = jnp.mean(zc * zc, axis=-1, keepdims=True)
    return zc * lax.rsqrt(var + LN_EPS) * g + b


def _silu(x):
    return x / (1.0 + jnp.exp(-x))


def _proj_kernel(*refs, scales):
    nw = len(scales)
    x_ref = refs[0]
    w_refs = refs[1:1 + nw]
    o_refs = refs[1 + nw:]
    xb = x_ref[...].astype(MXU_DTYPE)
    for w_ref, o_ref, sc in zip(w_refs, o_refs, scales):
        acc = _dot(xb, w_ref[...])
        if sc != 1.0:
            acc = acc * sc
        o_ref[...] = acc.astype(o_ref.dtype)


def _project(x, weights, scales):
    n, d = x.shape
    tm = min(PROJ_TM, n)
    in_specs = [pl.BlockSpec((tm, d), lambda i: (i, 0))]
    in_specs += [pl.BlockSpec(w.shape, lambda i: (0, 0)) for w in weights]
    out_specs = [pl.BlockSpec((tm, w.shape[1]), lambda i: (i, 0)) for w in weights]
    out_shape = [jax.ShapeDtypeStruct((n, w.shape[1]), MXU_DTYPE) for w in weights]
    return pl.pallas_call(
        functools.partial(_proj_kernel, scales=tuple(scales)),
        grid=(n // tm,),
        in_specs=in_specs,
        out_specs=out_specs,
        out_shape=out_shape,
        compiler_params=_params(("parallel",)),
        name="in_proj",
    )(x, *weights)


def _swa_kernel(sinks_ref, bias_ref, q_ref, kp_ref, kc_ref, vp_ref, vc_ref, o_ref):
    i = pl.program_id(1)
    q = q_ref[...]
    kp, kc, vp, vc = kp_ref[...], kc_ref[...], vp_ref[...], vc_ref[...]
    rows = SWA_GROUP * WINDOW
    head_of_row = lax.broadcasted_iota(jnp.int32, (rows, 1), 0) // WINDOW
    outs = []
    for kv in range(SWA_KV_HEADS):
        ks = slice(kv * HEAD_DIM, (kv + 1) * HEAD_DIM)
        kph, kch, vph, vch = kp[:, ks], kc[:, ks], vp[:, ks], vc[:, ks]
        h0 = kv * SWA_GROUP
        qg = jnp.concatenate([q[:, (h0 + g) * HEAD_DIM:(h0 + g + 1) * HEAD_DIM] for g in range(SWA_GROUP)],
                             axis=0)
        sink = jnp.zeros((rows, 1), jnp.float32) + sinks_ref[h0]
        for g in range(1, SWA_GROUP):
            sink = jnp.where(head_of_row == g, sinks_ref[h0 + g], sink)
        sp = jnp.where(i > 0, _dot_nt(qg, kph) + bias_ref[kv, 0], NEG)
        sc = _dot_nt(qg, kch) + bias_ref[kv, 1]
        m = jnp.maximum(jnp.maximum(jnp.max(sp, axis=-1, keepdims=True),
                                    jnp.max(sc, axis=-1, keepdims=True)), sink)
        pp = jnp.exp(sp - m)
        pc = jnp.exp(sc - m)
        denom = (jnp.sum(pp, axis=-1, keepdims=True) + jnp.sum(pc, axis=-1, keepdims=True)
                 + jnp.exp(sink - m))
        o = (_dot(pp.astype(MXU_DTYPE), vph) + _dot(pc.astype(MXU_DTYPE), vch)) / denom
        outs += [o[g * WINDOW:(g + 1) * WINDOW] for g in range(SWA_GROUP)]
    o_ref[...] = jnp.concatenate(outs, axis=-1).astype(o_ref.dtype)


def _swa_bias():
    slopes = np.asarray(_alibi_slopes(SWA_Q_HEADS), np.float64).reshape(SWA_KV_HEADS, SWA_GROUP)
    qi = np.arange(WINDOW)[:, None]
    kj = np.arange(WINDOW)[None, :]
    dist_c = qi - kj
    dist_p = dist_c + WINDOW
    out = np.empty((SWA_KV_HEADS, 2, SWA_GROUP, WINDOW, WINDOW), np.float32)
    for kv in range(SWA_KV_HEADS):
        for g in range(SWA_GROUP):
            out[kv, 0, g] = np.where(dist_p < WINDOW, -slopes[kv, g] * dist_p, NEG)
            out[kv, 1, g] = np.where(dist_c >= 0, -slopes[kv, g] * dist_c, NEG)
    return jnp.asarray(out.reshape(SWA_KV_HEADS, 2, SWA_GROUP * WINDOW, WINDOW))


def _swa_attention(q, k, v, sinks, batch, seq):
    n = batch * seq
    nb = seq // WINDOW
    qw, kw = q.shape[1], k.shape[1]
    cur = lambda b, i: (b * nb + i, 0)
    prev = lambda b, i: (b * nb + jnp.maximum(i - 1, 0), 0)
    bias = _swa_bias()
    return pl.pallas_call(
        _swa_kernel,
        grid=(batch, nb),
        in_specs=[
            pl.BlockSpec(memory_space=pltpu.SMEM),
            pl.BlockSpec(bias.shape, lambda b, i: (0, 0, 0, 0)),
            pl.BlockSpec((WINDOW, qw), cur),
            pl.BlockSpec((WINDOW, kw), prev),
            pl.BlockSpec((WINDOW, kw), cur),
            pl.BlockSpec((WINDOW, kw), prev),
            pl.BlockSpec((WINDOW, kw), cur),
        ],
        out_specs=pl.BlockSpec((WINDOW, qw), cur),
        out_shape=jax.ShapeDtypeStruct((n, qw), MXU_DTYPE),
        compiler_params=_params(("parallel", "parallel")),
        name="swa_attention",
    )(sinks, bias, q, k, k, v, v)


LOG2E = math.log2(math.e)
_BIAS_PIECES = 3


def _bf16_pieces(c):
    out, rest = [], np.float64(c)
    for _ in range(_BIAS_PIECES):
        piece = np.asarray(rest, np.float32).astype(jnp.bfloat16).astype(np.float64)
        out.append(float(piece))
        rest = rest - piece
    return out


def _alibi_templates(tq):
    idx = np.arange(tq)
    hi, lo = (idx // 256) * 256, idx % 256
    hw = 2 * HEAD_DIM
    qt = np.zeros((DIFF_HEADS, 2, tq, hw), np.float32)
    kt = np.zeros((DIFF_HEADS, 2, tq, hw), np.float32)
    for h, slope in enumerate(_alibi_slopes(DIFF_HEADS)):
        for c in range(2):
            base = HEAD_DIM if c == 0 else 0
            for p, piece in enumerate(_bf16_pieces(slope * LOG2E)):
                qt[h, c, :, base + p] = -hi
                kt[h, c, :, base + p] = piece
                qt[h, c, :, base + _BIAS_PIECES + p] = -lo
                kt[h, c, :, base + _BIAS_PIECES + p] = piece
                qt[h, c, :, base + 2 * _BIAS_PIECES + p] = piece
                kt[h, c, :, base + 2 * _BIAS_PIECES + p] = hi
                qt[h, c, :, base + 3 * _BIAS_PIECES + p] = piece
                kt[h, c, :, base + 3 * _BIAS_PIECES + p] = lo
    return jnp.asarray(qt, jnp.bfloat16), jnp.asarray(kt, jnp.bfloat16)


def _diff_kernel(slopes_ref, q_ref, k_ref, v_ref, qt_ref, kt_ref, ones_ref, dmask_ref, lam_ref, g_ref,
                 o_ref, qa_ref, m_ref, acc_ref, *, tq, lam_init):
    h = pl.program_id(1)
    qi = pl.program_id(2)
    slope2 = slopes_ref[h]
    hw = 2 * HEAD_DIM
    lane = lax.broadcasted_iota(jnp.int32, (1, hw), 1)
    first = lane < HEAD_DIM
    q = q_ref[...]
    qa_ref[0] = jnp.where(first, q, qt_ref[0, 0])
    qa_ref[1] = jnp.where(first, qt_ref[0, 1], q)
    m_ref[...] = jnp.full(m_ref.shape, NEG, jnp.float32)
    acc_ref[...] = jnp.zeros(acc_ref.shape, jnp.float32)
    ones = ones_ref[...]

    def softmax_pv(c, u, vaug, shift):
        m_old = m_ref[c]
        m_new = jnp.maximum(m_old, jnp.max(u, axis=-1, keepdims=True) + shift)
        p = jnp.exp2(u - (m_new - shift)).astype(MXU_DTYPE)
        alpha = jnp.exp2(m_old - m_new)
        acc_ref[c] = alpha * acc_ref[c] + _dot(p, vaug)
        m_ref[c] = m_new

    def process(ki, tiles, diagonal):
        vaugs, us, shifts = [], [], []
        for s in range(tiles):
            start = pl.multiple_of((ki + s) * tq, tq)
            k = k_ref[pl.ds(start, tq), :]
            v = v_ref[pl.ds(start, tq), :]
            vaugs.append(jnp.concatenate([v, ones], axis=1))
            kas = (jnp.where(first, k, kt_ref[0, 0]), jnp.where(first, kt_ref[0, 1], k))
            us.append([_dot_nt(qa_ref[c], kas[c]) for c in range(2)])
            shifts.append(slope2 * ((ki + s - qi) * tq).astype(jnp.float32))
        if diagonal:
            us = [[u + dmask_ref[...] for u in pair] for pair in us]
        for s in range(tiles):
            for c in range(2):
                softmax_pv(c, us[s][c], vaugs[s], shifts[s])

    def two_tiles(it, carry):
        process(2 * it, 2, False)
        return carry

    def one_tile(ki, carry):
        process(ki, 1, False)
        return carry

    pairs = qi // 2
    lax.fori_loop(0, pairs, two_tiles, 0)
    lax.fori_loop(2 * pairs, qi, one_tile, 0)
    process(qi, 1, True)

    lp = lam_ref[...]
    lam = (jnp.exp(jnp.sum(lp[0:1] * lp[1:2], axis=-1, keepdims=True))
           - jnp.exp(jnp.sum(lp[2:3] * lp[3:4], axis=-1, keepdims=True)) + lam_init)
    a0, a1 = acc_ref[0], acc_ref[1]
    o = a0[:, :hw] / a0[:, hw:] - lam * (a1[:, :hw] / a1[:, hw:])
    ms = jnp.mean(o * o, axis=-1, keepdims=True)
    o = o * lax.rsqrt(ms + LN_EPS) * g_ref[...] * (1.0 - lam_init)
    o_ref[...] = o.astype(o_ref.dtype)


def _diff_attention(q, k, v, lam_qk, subln_g, batch, seq, lam_init):
    n = batch * seq
    tq = min(DIFF_TQ, seq)
    nq = seq // tq
    hw = 2 * HEAD_DIM
    slopes2 = np.asarray(_alibi_slopes(DIFF_HEADS), np.float64) * LOG2E
    qt, kt = _alibi_templates(tq)
    ones_col = np.ones((tq, hw), np.float32)
    rel = np.arange(tq)[None, :] - np.arange(tq)[:, None]
    dmask = jnp.asarray(np.where(rel <= 0, 0.0, NEG).astype(np.float32))
    kernel = functools.partial(_diff_kernel, tq=tq, lam_init=lam_init)
    fixed = lambda b, h, i: (0, 0)
    per_head = lambda b, h, i: (h, 0, 0, 0)
    return pl.pallas_call(
        kernel,
        grid=(batch, DIFF_HEADS, nq),
        in_specs=[
            pl.BlockSpec(memory_space=pltpu.SMEM),
            pl.BlockSpec((tq, hw), lambda b, h, i: (b * nq + i, h)),
            pl.BlockSpec((seq, hw), lambda b, h, i: (b, h)),
            pl.BlockSpec((seq, hw), lambda b, h, i: (b, h)),
            pl.BlockSpec((1, 2, tq, hw), per_head),
            pl.BlockSpec((1, 2, tq, hw), per_head),
            pl.BlockSpec((tq, hw), fixed),
            pl.BlockSpec((tq, tq), fixed),
            pl.BlockSpec(lam_qk.shape, fixed),
            pl.BlockSpec(subln_g.shape, fixed),
        ],
        out_specs=pl.BlockSpec((tq, hw), lambda b, h, i: (b * nq + i, h)),
        out_shape=jax.ShapeDtypeStruct((n, DIFF_HEADS * hw), MXU_DTYPE),
        scratch_shapes=[
            pltpu.VMEM((2, tq, hw), MXU_DTYPE),
            pltpu.VMEM((2, tq, 1), jnp.float32),
            pltpu.VMEM((2, tq, 2 * hw), jnp.float32),
        ],
        compiler_params=_params(("parallel", "parallel", "arbitrary")),
        name="diff_attention",
    )(jnp.asarray(slopes2, jnp.float32), q, k, v, qt, kt, jnp.asarray(ones_col, MXU_DTYPE), dmask,
      lam_qk, subln_g)


def _memkv_kernel(mem_ref, w_ref, o_ref):
    o_ref[0] = _dot(mem_ref[...].astype(MXU_DTYPE), w_ref[0].astype(MXU_DTYPE)).astype(o_ref.dtype)


def _mem_kv(mem2d, w_mem_kv):
    depth, d, w = w_mem_kv.shape
    rows = mem2d.shape[0]
    return pl.pallas_call(
        _memkv_kernel,
        grid=(depth,),
        in_specs=[pl.BlockSpec((rows, d), lambda l: (0, 0)),
                  pl.BlockSpec((1, d, w), lambda l: (l, 0, 0))],
        out_specs=pl.BlockSpec((1, rows, w), lambda l: (l, 0, 0)),
        out_shape=jax.ShapeDtypeStruct((depth, rows, w), MXU_DTYPE),
        compiler_params=_params(("parallel",)),
        name="mem_kv_proj",
    )(mem2d, w_mem_kv)


ROW_SLAB = 4
_HI16 = 0xFFFF0000


def _bf16_bits_hi(x):
    b = lax.bitcast_convert_type(x, jnp.uint32)
    b = b + jnp.uint32(0x7FFF) + ((b >> 16) & jnp.uint32(1))
    return b & jnp.uint32(_HI16)


def _pack_rows(z, ref, base, t):
    half = z.shape[1] // 2
    words = _bf16_bits_hi(z[:, half:]) | (_bf16_bits_hi(z[:, :half]) >> 16)
    for s in range(ROW_SLAB):
        ref[pl.ds(base * ROW_SLAB + s, t, stride=ROW_SLAB), :] = words[:, s * 128:(s + 1) * 128]


def _unpack_rows(ref, base, t):
    lo, hi = [], []
    for s in range(ROW_SLAB):
        w = ref[pl.ds(base * ROW_SLAB + s, t, stride=ROW_SLAB), :]
        lo.append(lax.bitcast_convert_type(w << 16, jnp.float32))
        hi.append(lax.bitcast_convert_type(w & jnp.uint32(_HI16), jnp.float32))
    return lo + hi


def _outproj_kernel(mix_ref, qm_ref, kv_ref, wo_ref, x_ref, g_ref, b_ref, o_ref, op_ref):
    qm = qm_ref[...]
    kvm = kv_ref[0]
    mw = MEM_HEADS * HEAD_DIM
    heads = [slice(h * HEAD_DIM, (h + 1) * HEAD_DIM) for h in range(MEM_HEADS)]
    scores = [_dot_nt(qm[:, hs], kvm[:, hs]) for hs in heads]
    tops = [jnp.max(s, axis=-1, keepdims=True) for s in scores]
    probs = [jnp.exp(s - m) for s, m in zip(scores, tops)]
    sums = [jnp.sum(p, axis=-1, keepdims=True) for p in probs]
    outs = [_dot(p.astype(MXU_DTYPE), kvm[:, mw + hs.start: mw + hs.stop]) / l
            for p, hs, l in zip(probs, heads, sums)]
    mem_out = jnp.concatenate(outs, axis=-1).astype(MXU_DTYPE)
    mixw = mix_ref.shape[1]
    y = _dot(mix_ref[...], wo_ref[0:mixw, :]) + _dot(mem_out, wo_ref[mixw:, :])
    z = DN_ALPHA * x_ref[...] + y
    zn = _layer_norm(z, g_ref[...], b_ref[...])
    o_ref[...] = zn
    _pack_rows(zn, op_ref, 0, zn.shape[0])


def _out_proj(mix, qm, kv_l, w_out, x, ln_g, ln_b, batch, seq):
    n, d = x.shape
    tm = min(PROJ_TM, seq)
    per_b = seq // tm
    mlen, kvw = kv_l.shape[1], kv_l.shape[2]
    row = lambda i: (i, 0)
    whole = lambda i: (0, 0)
    xn, xp = pl.pallas_call(
        _outproj_kernel,
        grid=(n // tm,),
        in_specs=[
            pl.BlockSpec((tm, mix.shape[1]), row),
            pl.BlockSpec((tm, qm.shape[1]), row),
            pl.BlockSpec((1, mlen, kvw), lambda i: (i // per_b, 0, 0)),
            pl.BlockSpec(w_out.shape, whole),
            pl.BlockSpec((tm, d), row),
            pl.BlockSpec((1, d), whole),
            pl.BlockSpec((1, d), whole),
        ],
        out_specs=[pl.BlockSpec((tm, d), row), pl.BlockSpec((tm * ROW_SLAB, 128), row)],
        out_shape=[jax.ShapeDtypeStruct((n, d), jnp.float32),
                   jax.ShapeDtypeStruct((n * ROW_SLAB, 128), jnp.uint32)],
        compiler_params=_params(("parallel",)),
        name="out_proj_ln",
    )(mix, qm, kv_l, w_out, x, ln_g, ln_b)
    return xn, xp.reshape(n, ROW_SLAB, 128)


def _split_hi_lo(a):
    hi = a.astype(jnp.bfloat16)
    lo = (a - hi.astype(jnp.float32)).astype(jnp.bfloat16)
    return hi, lo


def _route_kernel(x_ref, w_ref, bias_ref, eid_ref, tri_ref, gate_ref, rank_ref, cnt_ref, carry_ref):
    i = pl.program_id(0)

    @pl.when(i == 0)
    def _():
        carry_ref[...] = jnp.zeros(carry_ref.shape, jnp.float32)

    xh, xl = _split_hi_lo(x_ref[...])
    wh, wl = _split_hi_lo(w_ref[...])
    logits = _dot_nt(wh, xh) + _dot_nt(wh, xl) + _dot_nt(wl, xh)
    aff = 1.0 / (1.0 + jnp.exp(-logits))
    sel = aff + bias_ref[...]
    t = sel.shape[1]

    sel3 = sel.reshape(PER_GROUP, N_GROUPS, t)
    jidx = lax.broadcasted_iota(jnp.int32, (PER_GROUP, N_GROUPS, t), 0)
    m1 = jnp.max(sel3, axis=0, keepdims=True)
    first = jnp.min(jnp.where(sel3 == m1, jidx, PER_GROUP), axis=0, keepdims=True)
    m2 = jnp.max(jnp.where(jidx == first, -jnp.inf, sel3), axis=0, keepdims=True)
    gscore = (m1 + m2)[0]

    gid = lax.broadcasted_iota(jnp.int32, (N_GROUPS, t), 0)
    grank = jnp.zeros((N_GROUPS, t), jnp.int32)
    for g2 in range(N_GROUPS):
        other = gscore[g2:g2 + 1, :]
        beats = jnp.logical_or(other > gscore, jnp.logical_and(other == gscore, g2 < gid))
        grank = grank + beats.astype(jnp.int32)
    gmask = grank < TOPK_GROUPS
    emask = jnp.broadcast_to(gmask[None], (PER_GROUP, N_GROUPS, t)).reshape(N_EXPERTS, t)
    val = jnp.where(emask, sel, -jnp.inf)

    eid = eid_ref[...]
    chosen = jnp.zeros((N_EXPERTS, t), jnp.bool_)
    for _ in range(TOP_K):
        best = jnp.max(val, axis=0, keepdims=True)
        pick = jnp.min(jnp.where(val == best, eid, N_EXPERTS), axis=0, keepdims=True)
        hit = eid == pick
        chosen = jnp.logical_or(chosen, hit)
        val = jnp.where(hit, -jnp.inf, val)
    picked = jnp.where(chosen, aff, 0.0)
    gate_ref[...] = picked / jnp.sum(picked, axis=0, keepdims=True) * ROUTED_SCALE

    ind = chosen.astype(jnp.bfloat16)
    prefix = _dot(ind, tri_ref[...])
    carry = carry_ref[...]
    rank_ref[...] = jnp.where(chosen, prefix + carry, -1.0)
    carry = carry + jnp.sum(chosen.astype(jnp.float32), axis=1, keepdims=True)
    carry_ref[...] = carry
    cnt_ref[...] = carry


def _route(x, w_router_t, bias_col, eid_col):
    n, d = x.shape
    t = min(ROUTE_T, n)
    tri = jnp.asarray(np.triu(np.ones((t, t), np.float32), 1), jnp.bfloat16)
    whole = lambda i: (0, 0)
    return pl.pallas_call(
        _route_kernel,
        grid=(n // t,),
        in_specs=[
            pl.BlockSpec((t, d), lambda i: (i, 0)),
            pl.BlockSpec((N_EXPERTS, d), whole),
            pl.BlockSpec((N_EXPERTS, 1), whole),
            pl.BlockSpec((N_EXPERTS, 1), whole),
            pl.BlockSpec((t, t), whole),
        ],
        out_specs=[
            pl.BlockSpec((N_EXPERTS, t), lambda i: (0, i)),
            pl.BlockSpec((N_EXPERTS, t), lambda i: (0, i)),
            pl.BlockSpec((N_EXPERTS, 1), whole),
        ],
        out_shape=[
            jax.ShapeDtypeStruct((N_EXPERTS, n), jnp.float32),
            jax.ShapeDtypeStruct((N_EXPERTS, n), jnp.float32),
            jax.ShapeDtypeStruct((N_EXPERTS, 1), jnp.float32),
        ],
        scratch_shapes=[pltpu.VMEM((N_EXPERTS, 1), jnp.float32)],
        compiler_params=_params(("arbitrary",)),
        name="moe_route",
    )(x, w_router_t, bias_col, eid_col, tri)


def _dest_kernel(gate_ref, rank_ref, pstart_ref, ltri_ref, dest_ref, g_ref):
    rank = rank_ref[...]
    chosen = rank >= 0.0
    slot = _dot(ltri_ref[...], chosen.astype(jnp.bfloat16))
    row = rank + pstart_ref[...]
    gate = gate_ref[...]
    dests, gates = [], []
    for k in range(TOP_K):
        hit = jnp.logical_and(chosen, slot == float(k))
        dests.append(jnp.sum(jnp.where(hit, row, 0.0), axis=0, keepdims=True))
        gates.append(jnp.sum(jnp.where(hit, gate, 0.0), axis=0, keepdims=True))
    dest_ref[...] = jnp.concatenate(dests, axis=0).astype(jnp.int32)
    g_ref[...] = jnp.concatenate(gates, axis=0)


def _dest(gate_dense, rank_dense, pstart_col):
    e, n = gate_dense.shape
    t = min(ROUTE_T, n)
    ltri = jnp.asarray(np.tril(np.ones((e, e), np.float32), -1), jnp.bfloat16)
    whole = lambda i: (0, 0)
    tile = lambda i: (0, i)
    return pl.pallas_call(
        _dest_kernel,
        grid=(n // t,),
        in_specs=[pl.BlockSpec((e, t), tile), pl.BlockSpec((e, t), tile),
                  pl.BlockSpec((e, 1), whole), pl.BlockSpec((e, e), whole)],
        out_specs=[pl.BlockSpec((TOP_K, t), tile), pl.BlockSpec((TOP_K, t), tile)],
        out_shape=[jax.ShapeDtypeStruct((TOP_K, n), jnp.int32),
                   jax.ShapeDtypeStruct((TOP_K, n), jnp.float32)],
        compiler_params=_params(("parallel",)),
        name="moe_dest",
    )(gate_dense, rank_dense, pstart_col, ltri)


def _dispatch_kernel(dest_ref, padlo_ref, padn_ref, nu_ref, xp_ref, xs_ref, zero_ref, sem, zsem, *,
                     t_tile, nblk):
    i = pl.program_id(0)

    @pl.when(i == 0)
    def _():
        zero_ref[...] = jnp.zeros(zero_ref.shape, zero_ref.dtype)

        def for_each_pad_copy(e, fn):
            off = padlo_ref[e]
            cnt = padn_ref[e]
            for bit in reversed(range(MOE_BLOCK.bit_length() - 1)):
                size = 1 << bit
                on = (cnt & size) != 0

                @pl.when(on)
                def _():
                    fn(pltpu.make_async_copy(zero_ref.at[pl.ds(0, size)], xs_ref.at[pl.ds(off, size)], zsem))
                off = off + jnp.where(on, size, 0)

        def tail_copy(b):
            return pltpu.make_async_copy(zero_ref, xs_ref.at[pl.ds(b * MOE_BLOCK, MOE_BLOCK)], zsem)

        def start_pad(e, c):
            for_each_pad_copy(e, lambda cp: cp.start())
            return c

        def wait_pad(e, c):
            for_each_pad_copy(e, lambda cp: cp.wait())
            return c

        def start_tail(b, c):
            tail_copy(b).start()
            return c

        def wait_tail(b, c):
            tail_copy(b).wait()
            return c

        lax.fori_loop(0, N_EXPERTS, start_pad, 0)
        lax.fori_loop(nu_ref[0], nblk, start_tail, 0)
        lax.fori_loop(0, N_EXPERTS, wait_pad, 0)
        lax.fori_loop(nu_ref[0], nblk, wait_tail, 0)

    def issue(t, carry):
        for k in range(TOP_K):
            d = dest_ref[k, t]
            pltpu.make_async_copy(xp_ref.at[t], xs_ref.at[d], sem).start(priority=k % 2)
        return carry

    lax.fori_loop(0, t_tile, issue, 0)
    for k in range(TOP_K):
        pltpu.make_async_copy(xp_ref, xs_ref.at[pl.ds(0, t_tile)], sem).wait()


def _dispatch(xp, dest, pad_lo, pad_n, n_used, n_rows):
    n = xp.shape[0]
    t = min(DISPATCH_T, n)
    nblk = n_rows // MOE_BLOCK
    smem = pl.BlockSpec(memory_space=pltpu.SMEM)
    return pl.pallas_call(
        functools.partial(_dispatch_kernel, t_tile=t, nblk=nblk),
        grid=(n // t,),
        in_specs=[
            pl.BlockSpec((TOP_K, t), lambda i: (0, i), memory_space=pltpu.SMEM),
            smem, smem, smem,
            pl.BlockSpec((t, ROW_SLAB, 128), lambda i: (i, 0, 0)),
        ],
        out_specs=pl.BlockSpec(memory_space=pl.ANY),
        out_shape=jax.ShapeDtypeStruct((n_rows, ROW_SLAB, 128), jnp.uint32),
        scratch_shapes=[pltpu.VMEM((MOE_BLOCK, ROW_SLAB, 128), jnp.uint32),
                        pltpu.SemaphoreType.DMA(()), pltpu.SemaphoreType.DMA(())],
        compiler_params=_params(("arbitrary",)),
        name="moe_dispatch",
    )(dest, pad_lo, pad_n, n_used, xp)


def _experts_kernel(be_ref, nu_ref, xs_ref, wg_ref, wu_ref, wd_ref, ys_ref, wgc_ref, wuc_ref, wdc_ref):
    b = pl.program_id(0)
    new_expert = jnp.logical_or(b == 0, be_ref[b] != be_ref[jnp.maximum(b - 1, 0)])

    @pl.when(jnp.logical_and(b < nu_ref[0], new_expert))
    def _():
        wgc_ref[...] = wg_ref[0, 0].astype(MXU_DTYPE)
        wuc_ref[...] = wu_ref[0, 0].astype(MXU_DTYPE)
        wdc_ref[...] = wd_ref[0, 0].astype(MXU_DTYPE)

    @pl.when(b < nu_ref[0])
    def _():
        xb = jnp.concatenate(_unpack_rows(xs_ref, 0, MOE_BLOCK), axis=1).astype(MXU_DTYPE)
        g = _dot(xb, wgc_ref[...])
        u = _dot(xb, wuc_ref[...])
        hmid = (_silu(g) * u).astype(MXU_DTYPE)
        _pack_rows(_dot(hmid, wdc_ref[...]), ys_ref, 0, MOE_BLOCK)

    @pl.when(b >= nu_ref[0])
    def _():
        ys_ref[...] = jnp.zeros(ys_ref.shape, ys_ref.dtype)


def _experts(xs2d, block_e, n_used, w_gate, w_up, w_down, layer):
    d, hid = w_gate.shape[2], w_gate.shape[3]
    blk_rows = MOE_BLOCK * ROW_SLAB
    nblk = xs2d.shape[0] // blk_rows
    blk = lambda b, be, nu: (b, 0)
    used_blk = lambda b, be, nu: (jnp.minimum(b, nu[0] - 1), 0)
    expert = lambda b, be, nu: (layer, be[b], 0, 0)
    grid_spec = pltpu.PrefetchScalarGridSpec(
        num_scalar_prefetch=2,
        grid=(nblk,),
        in_specs=[
            pl.BlockSpec((blk_rows, 128), used_blk),
            pl.BlockSpec((1, 1, d, hid), expert),
            pl.BlockSpec((1, 1, d, hid), expert),
            pl.BlockSpec((1, 1, hid, d), expert),
        ],
        out_specs=pl.BlockSpec((blk_rows, 128), blk),
        scratch_shapes=[pltpu.VMEM((d, hid), MXU_DTYPE), pltpu.VMEM((d, hid), MXU_DTYPE),
                        pltpu.VMEM((hid, d), MXU_DTYPE)],
    )
    return pl.pallas_call(
        _experts_kernel,
        grid_spec=grid_spec,
        out_shape=jax.ShapeDtypeStruct(xs2d.shape, jnp.uint32),
        compiler_params=_params(("arbitrary",)),
        name="moe_experts",
    )(block_e, n_used, xs2d, w_gate, w_up, w_down)


def _combine_kernel(dest_ref, dnext_ref, g_ref, x_ref, wsg_ref, wsu_ref, wsd_ref, lg_ref, lb_ref, ys_ref,
                    ys2d_ref, o_ref, buf_ref, sem, *, t_tile):
    i = pl.program_id(0)
    last = pl.num_programs(0) - 1
    slot_rows = TOP_K * t_tile
    slot = lax.rem(i, 2)

    def gather(dref, s, unrolled):
        base = s * slot_rows

        def issue(t, carry):
            for k in range(TOP_K):
                r = pl.multiple_of((base + k * t_tile + t) * ROW_SLAB, ROW_SLAB)
                pltpu.make_async_copy(ys_ref.at[dref[k, t]], buf_ref.at[pl.ds(r, ROW_SLAB), :],
                                      sem).start(priority=k % 2)
            return carry

        if unrolled:
            for t in range(t_tile):
                issue(t, 0)
        else:
            lax.fori_loop(0, t_tile, issue, 0)

    def wait(s):
        n = t_tile * ROW_SLAB
        for k in range(TOP_K):
            r = pl.multiple_of((s * slot_rows + k * t_tile) * ROW_SLAB, ROW_SLAB)
            pltpu.make_async_copy(ys2d_ref.at[pl.ds(0, n), :], buf_ref.at[pl.ds(r, n), :], sem).wait()

    @pl.when(i == 0)
    def _():
        gather(dest_ref, 0, False)

    wait(slot)
    gather(dnext_ref, 1 - slot, True)

    x = x_ref[...]
    xb = x.astype(MXU_DTYPE)
    hmid = (_silu(_dot(xb, wsg_ref[...])) * _dot(xb, wsu_ref[...])).astype(MXU_DTYPE)
    z = DN_ALPHA * x + _dot(hmid, wsd_ref[...])

    gt = jnp.transpose(g_ref[...])
    cols = None
    for k in range(TOP_K):
        tiles = _unpack_rows(buf_ref, slot * slot_rows + k * t_tile, t_tile)
        gk = gt[:, k:k + 1]
        cols = [tl * gk for tl in tiles] if cols is None else [c + tl * gk for c, tl in zip(cols, tiles)]
    z = z + jnp.concatenate(cols, axis=1)
    o_ref[...] = _layer_norm(z, lg_ref[...], lb_ref[...])

    @pl.when(i == last)
    def _():
        wait(1 - slot)


def _combine(x, dest, gates, ys, ws_gate, ws_up, ws_down, ln_g, ln_b):
    n, d = x.shape
    t = min(COMBINE_T, n)
    steps = n // t
    whole = lambda i: (0, 0)
    tile = lambda i: (0, i)
    nxt = lambda i: (0, jnp.minimum(i + 1, steps - 1))
    return pl.pallas_call(
        functools.partial(_combine_kernel, t_tile=t),
        grid=(steps,),
        in_specs=[
            pl.BlockSpec((TOP_K, t), tile, memory_space=pltpu.SMEM),
            pl.BlockSpec((TOP_K, t), nxt, memory_space=pltpu.SMEM),
            pl.BlockSpec((TOP_K, t), tile),
            pl.BlockSpec((t, d), lambda i: (i, 0)),
            pl.BlockSpec(ws_gate.shape, whole),
            pl.BlockSpec(ws_up.shape, whole),
            pl.BlockSpec(ws_down.shape, whole),
            pl.BlockSpec((1, d), whole),
            pl.BlockSpec((1, d), whole),
            pl.BlockSpec(memory_space=pl.ANY),
            pl.BlockSpec(memory_space=pl.ANY),
        ],
        out_specs=pl.BlockSpec((t, d), lambda i: (i, 0)),
        out_shape=jax.ShapeDtypeStruct((n, d), jnp.float32),
        scratch_shapes=[pltpu.VMEM((2 * TOP_K * t * ROW_SLAB, 128), jnp.uint32), pltpu.SemaphoreType.DMA(())],
        compiler_params=_params(("arbitrary",)),
        name="moe_combine",
    )(dest, dest, gates, x, ws_gate, ws_up, ws_down, ln_g, ln_b, ys, ys.reshape(-1, 128))


def _moe(x, xp, w_router_t, bias_col, eid_col, inv_perm, w_gate, w_up, w_down, layer, ws_gate, ws_up, ws_down,
         ln_g, ln_b):
    n, d = x.shape
    gate_dense, rank_dense, counts = _route(x, w_router_t, bias_col, eid_col)
    cnt = counts[:, 0].astype(jnp.int32)[inv_perm]
    padded = (cnt + MOE_BLOCK - 1) // MOE_BLOCK * MOE_BLOCK
    pends = jnp.cumsum(padded)
    pstarts = pends - padded
    nk = n * TOP_K
    n_rows = (nk + N_EXPERTS * (MOE_BLOCK - 1) + MOE_BLOCK - 1) // MOE_BLOCK * MOE_BLOCK
    nblk = n_rows // MOE_BLOCK
    blk_start = jnp.arange(nblk, dtype=jnp.int32) * MOE_BLOCK
    block_e = jnp.minimum(jnp.sum(blk_start[:, None] >= pends[None, :], axis=1), N_EXPERTS - 1).astype(jnp.int32)
    n_used = (pends[-1:] // MOE_BLOCK).astype(jnp.int32)
    perm = jnp.argsort(inv_perm)
    pstart_col = pstarts[perm].astype(jnp.float32)[:, None]
    dest, gates = _dest(gate_dense, rank_dense, pstart_col)
    xs = _dispatch(xp, dest, (pstarts + cnt).astype(jnp.int32), (padded - cnt).astype(jnp.int32), n_used, n_rows)
    ys = _experts(xs.reshape(n_rows * ROW_SLAB, 128), block_e, n_used, w_gate, w_up, w_down, layer)
    return _combine(x, dest, gates, ys.reshape(n_rows, ROW_SLAB, 128), ws_gate, ws_up, ws_down, ln_g, ln_b)


def kernel(x, mem, w_in_a, sinks, w_in_b, lambda_qk, subln_g, w_mem_kv, w_out, ln_g, ln_b, w_router,
           router_bias, w_gate, w_up, w_down, w_shared_gate, w_shared_up, w_shared_down):
    batch, seq, d = x.shape
    n = batch * seq
    cdt = MXU_DTYPE
    qa, kva = SWA_Q_HEADS * HEAD_DIM, SWA_KV_HEADS * HEAD_DIM
    qb, vb = DIFF_HEADS * 2 * HEAD_DIM, DIFF_HEADS * DIFF_V_DIM

    rows = np.arange(N_EXPERTS)
    perm = (rows % N_GROUPS) * PER_GROUP + rows // N_GROUPS
    inv_perm = jnp.asarray(np.argsort(perm), jnp.int32)
    eid_col = jnp.asarray(perm[:, None], jnp.int32)

    xf = x.reshape(n, d)
    kv_all = _mem_kv(mem.reshape(-1, d), w_mem_kv)
    mlen = mem.shape[1]
    kv_all = kv_all.reshape(DEPTH * batch, mlen, kv_all.shape[-1])

    for i in range(DEPTH):
        j = i // 2
        if i % 2 == 0:
            w = w_in_a[j].astype(cdt)
            ws = [w[:, :qa], w[:, qa:qa + kva], w[:, qa + kva:qa + 2 * kva], w[:, qa + 2 * kva:]]
            q, k, v, qm = _project(xf, ws, (QK_SCALE, 1.0, 1.0, QK_SCALE))
            mix = _swa_attention(q, k, v, sinks[j], batch, seq)
        else:
            w = w_in_b[j].astype(cdt)
            ws = [w[:, :qb], w[:, qb:2 * qb], w[:, 2 * qb:2 * qb + vb], w[:, 2 * qb + vb:]]
            q, k, v, qm = _project(xf, ws, (QK_SCALE * LOG2E, 1.0, 1.0, QK_SCALE))
            lam_init = 0.8 - 0.6 * math.exp(-0.3 * i)
            mix = _diff_attention(q, k, v, lambda_qk[j], subln_g[j][None, :], batch, seq, lam_init)
        kv_l = kv_all[i * batch:(i + 1) * batch]
        xf, xp = _out_proj(mix, qm, kv_l, w_out[i].astype(cdt), xf, ln_g[i, 0][None, :], ln_b[i, 0][None, :],
                           batch, seq)
        w_router_t = jnp.transpose(w_router[i])[perm]
        bias_col = router_bias[i][perm][:, None]
        xf = _moe(xf, xp, w_router_t, bias_col, eid_col, inv_perm,
                  w_gate, w_up, w_down, i,
                  w_shared_gate[i].astype(cdt), w_shared_up[i].astype(cdt), w_shared_down[i].astype(cdt),
                  ln_g[i, 1][None, :], ln_b[i, 1][None, :])
    return xf.reshape(batch, seq, d)
```

```python
import functools
import math

import jax
import jax.numpy as jnp
import numpy as np
from jax import lax
from jax.experimental import pallas as pl
from jax.experimental.pallas import tpu as pltpu

HEAD_DIM = 64
MEM_HEADS = 4
SWA_Q_HEADS = 12
SWA_KV_HEADS = 3
SWA_GROUP = SWA_Q_HEADS // SWA_KV_HEADS
WINDOW = 128
DIFF_HEADS = 6
DIFF_V_DIM = 2 * HEAD_DIM
N_EXPERTS = 64
TOP_K = 8
N_GROUPS = 8
PER_GROUP = N_EXPERTS // N_GROUPS
TOPK_GROUPS = 4
ROUTED_SCALE = 2.5
MOE_BLOCK = 512
DEPTH = 4
DN_ALPHA = (2.0 * DEPTH) ** 0.25
LN_EPS = 1e-5
NEG = -1e30
QK_SCALE = HEAD_DIM ** -0.5

MXU_DTYPE = jnp.bfloat16

PROJ_TM = 512
DIFF_TQ = 1024
ROUTE_T = 512
DISPATCH_T = 256
COMBINE_T = 256
VMEM_LIMIT = 56 * 1024 * 1024


def _alibi_slopes(n):
    def pow2_slopes(m):
        start = 2.0 ** (-8.0 / m)
        return [start ** (i + 1) for i in range(m)]
    if (n & (n - 1)) == 0:
        return pow2_slopes(n)
    c = 2 ** int(math.floor(math.log2(n)))
    return pow2_slopes(c) + pow2_slopes(2 * c)[0::2][: n - c]


def _dot(a, b):
    return jnp.dot(a, b, preferred_element_type=jnp.float32)


def _dot_nt(a, b):
    return lax.dot_general(a, b, (((1,), (1,)), ((), ())), preferred_element_type=jnp.float32)


def _params(sem):
    return pltpu.CompilerParams(dimension_semantics=sem, vmem_limit_bytes=VMEM_LIMIT)


def _layer_norm(z, g, b):
    mu = jnp.mean(z, axis=-1, keepdims=True)
    zc = z - mu
    var = jnp.mean(zc * zc, axis=-1, keepdims=True)
    return zc * lax.rsqrt(var + LN_EPS) * g + b


def _silu(x):
    return x / (1.0 + jnp.exp(-x))


def _proj_kernel(*refs, scales):
    nw = len(scales)
    x_ref = refs[0]
    w_refs = refs[1:1 + nw]
    o_refs = refs[1 + nw:]
    xb = x_ref[...].astype(MXU_DTYPE)
    for w_ref, o_ref, sc in zip(w_refs, o_refs, scales):
        acc = _dot(xb, w_ref[...])
        if sc != 1.0:
            acc = acc * sc
        o_ref[...] = acc.astype(o_ref.dtype)


def _project(x, weights, scales):
    n, d = x.shape
    tm = min(PROJ_TM, n)
    in_specs = [pl.BlockSpec((tm, d), lambda i: (i, 0))]
    in_specs += [pl.BlockSpec(w.shape, lambda i: (0, 0)) for w in weights]
    out_specs = [pl.BlockSpec((tm, w.shape[1]), lambda i: (i, 0)) for w in weights]
    out_shape = [jax.ShapeDtypeStruct((n, w.shape[1]), MXU_DTYPE) for w in weights]
    return pl.pallas_call(
        functools.partial(_proj_kernel, scales=tuple(scales)),
        grid=(n // tm,),
        in_specs=in_specs,
        out_specs=out_specs,
        out_shape=out_shape,
        compiler_params=_params(("parallel",)),
        name="in_proj",
    )(x, *weights)


def _swa_kernel(sinks_ref, bias_ref, q_ref, kp_ref, kc_ref, vp_ref, vc_ref, o_ref):
    i = pl.program_id(1)
    q = q_ref[...]
    kp, kc, vp, vc = kp_ref[...], kc_ref[...], vp_ref[...], vc_ref[...]
    rows = SWA_GROUP * WINDOW
    head_of_row = lax.broadcasted_iota(jnp.int32, (rows, 1), 0) // WINDOW
    vpad = jnp.concatenate([jnp.zeros((WINDOW, HEAD_DIM), MXU_DTYPE),
                            jnp.ones((WINDOW, 2 * HEAD_DIM), MXU_DTYPE)], axis=1)
    outs = []
    for kv in range(SWA_KV_HEADS):
        ks = slice(kv * HEAD_DIM, (kv + 1) * HEAD_DIM)
        kph, kch, vph, vch = kp[:, ks], kc[:, ks], vp[:, ks], vc[:, ks]
        h0 = kv * SWA_GROUP
        qg = jnp.concatenate([q[:, (h0 + g) * HEAD_DIM:(h0 + g + 1) * HEAD_DIM] for g in range(SWA_GROUP)],
                             axis=0)
        sink = jnp.zeros((rows, 1), jnp.float32) + sinks_ref[h0]
        for g in range(1, SWA_GROUP):
            sink = jnp.where(head_of_row == g, sinks_ref[h0 + g], sink)
        sp = jnp.where(i > 0, _dot_nt(qg, kph) + bias_ref[kv, 0], NEG)
        sc = _dot_nt(qg, kch) + bias_ref[kv, 1]
        m = jnp.maximum(jnp.maximum(jnp.max(sp, axis=-1, keepdims=True),
                                    jnp.max(sc, axis=-1, keepdims=True)), sink)
        pp = jnp.exp(sp - m).astype(MXU_DTYPE)
        pc = jnp.exp(sc - m).astype(MXU_DTYPE)
        ov = _dot(pp, jnp.concatenate([vph, vpad], axis=1)) + _dot(pc, jnp.concatenate([vch, vpad], axis=1))
        denom = ov[:, 2 * HEAD_DIM:3 * HEAD_DIM] + jnp.exp(sink - m)
        o = ov[:, :HEAD_DIM] / denom
        outs += [o[g * WINDOW:(g + 1) * WINDOW] for g in range(SWA_GROUP)]
    o_ref[...] = jnp.concatenate(outs, axis=-1).astype(o_ref.dtype)


def _swa_bias():
    slopes = np.asarray(_alibi_slopes(SWA_Q_HEADS), np.float64).reshape(SWA_KV_HEADS, SWA_GROUP)
    qi = np.arange(WINDOW)[:, None]
    kj = np.arange(WINDOW)[None, :]
    dist_c = qi - kj
    dist_p = dist_c + WINDOW
    out = np.empty((SWA_KV_HEADS, 2, SWA_GROUP, WINDOW, WINDOW), np.float32)
    for kv in range(SWA_KV_HEADS):
        for g in range(SWA_GROUP):
            out[kv, 0, g] = np.where(dist_p < WINDOW, -slopes[kv, g] * dist_p, NEG)
            out[kv, 1, g] = np.where(dist_c >= 0, -slopes[kv, g] * dist_c, NEG)
    return jnp.asarray(out.reshape(SWA_KV_HEADS, 2, SWA_GROUP * WINDOW, WINDOW))


def _swa_attention(q, k, v, sinks, batch, seq):
    n = batch * seq
    nb = seq // WINDOW
    qw, kw = q.shape[1], k.shape[1]
    cur = lambda b, i: (b * nb + i, 0)
    prev = lambda b, i: (b * nb + jnp.maximum(i - 1, 0), 0)
    bias = _swa_bias()
    return pl.pallas_call(
        _swa_kernel,
        grid=(batch, nb),
        in_specs=[
            pl.BlockSpec(memory_space=pltpu.SMEM),
            pl.BlockSpec(bias.shape, lambda b, i: (0, 0, 0, 0)),
            pl.BlockSpec((WINDOW, qw), cur),
            pl.BlockSpec((WINDOW, kw), prev),
            pl.BlockSpec((WINDOW, kw), cur),
            pl.BlockSpec((WINDOW, kw), prev),
            pl.BlockSpec((WINDOW, kw), cur),
        ],
        out_specs=pl.BlockSpec((WINDOW, qw), cur),
        out_shape=jax.ShapeDtypeStruct((n, qw), MXU_DTYPE),
        compiler_params=_params(("parallel", "parallel")),
        name="swa_attention",
    )(sinks, bias, q, k, k, v, v)


LOG2E = math.log2(math.e)
_BIAS_PIECES = 3


def _bf16_pieces(c):
    out, rest = [], np.float64(c)
    for _ in range(_BIAS_PIECES):
        piece = np.asarray(rest, np.float32).astype(jnp.bfloat16).astype(np.float64)
        out.append(float(piece))
        rest = rest - piece
    return out


def _alibi_templates(tq):
    idx = np.arange(tq)
    hi, lo = (idx // 256) * 256, idx % 256
    hw = 2 * HEAD_DIM
    qt = np.zeros((DIFF_HEADS, 2, tq, hw), np.float32)
    kt = np.zeros((DIFF_HEADS, 2, tq, hw), np.float32)
    for h, slope in enumerate(_alibi_slopes(DIFF_HEADS)):
        for c in range(2):
            base = HEAD_DIM if c == 0 else 0
            for p, piece in enumerate(_bf16_pieces(slope * LOG2E)):
                qt[h, c, :, base + p] = -hi
                kt[h, c, :, base + p] = piece
                qt[h, c, :, base + _BIAS_PIECES + p] = -lo
                kt[h, c, :, base + _BIAS_PIECES + p] = piece
                qt[h, c, :, base + 2 * _BIAS_PIECES + p] = piece
                kt[h, c, :, base + 2 * _BIAS_PIECES + p] = hi
                qt[h, c, :, base + 3 * _BIAS_PIECES + p] = piece
                kt[h, c, :, base + 3 * _BIAS_PIECES + p] = lo
    return jnp.asarray(qt, jnp.bfloat16), jnp.asarray(kt, jnp.bfloat16)


def _diff_kernel(slopes_ref, q_ref, k_ref, v_ref, qt_ref, kt_ref, ones_ref, dmask_ref, lam_ref, g_ref,
                 o_ref, qa_ref, m_ref, acc_ref, *, tq, lam_init):
    h = pl.program_id(1)
    qi = pl.program_id(2)
    slope2 = slopes_ref[h]
    hw = 2 * HEAD_DIM
    lane = lax.broadcasted_iota(jnp.int32, (1, hw), 1)
    first = lane < HEAD_DIM
    q = q_ref[...]
    qa_ref[0] = jnp.where(first, q, qt_ref[0, 0])
    qa_ref[1] = jnp.where(first, qt_ref[0, 1], q)
    m_ref[...] = jnp.full(m_ref.shape, NEG, jnp.float32)
    acc_ref[...] = jnp.zeros(acc_ref.shape, jnp.float32)
    ones = ones_ref[...]

    def softmax_pv(c, u, vaug, shift):
        m_old = m_ref[c]
        m_new = jnp.maximum(m_old, jnp.max(u, axis=-1, keepdims=True) + shift)
        p = jnp.exp2(u - (m_new - shift)).astype(MXU_DTYPE)
        alpha = jnp.exp2(m_old - m_new)
        acc_ref[c] = alpha * acc_ref[c] + _dot(p, vaug)
        m_ref[c] = m_new

    def process(ki, tiles, diagonal):
        vaugs, us, shifts = [], [], []
        for s in range(tiles):
            start = pl.multiple_of((ki + s) * tq, tq)
            k = k_ref[pl.ds(start, tq), :]
            v = v_ref[pl.ds(start, tq), :]
            vaugs.append(jnp.concatenate([v, ones], axis=1))
            kas = (jnp.where(first, k, kt_ref[0, 0]), jnp.where(first, kt_ref[0, 1], k))
            us.append([_dot_nt(qa_ref[c], kas[c]) for c in range(2)])
            shifts.append(slope2 * ((ki + s - qi) * tq).astype(jnp.float32))
        if diagonal:
            us = [[u + dmask_ref[...] for u in pair] for pair in us]
        for s in range(tiles):
            for c in range(2):
                softmax_pv(c, us[s][c], vaugs[s], shifts[s])

    def two_tiles(it, carry):
        process(2 * it, 2, False)
        return carry

    def one_tile(ki, carry):
        process(ki, 1, False)
        return carry

    pairs = qi // 2
    lax.fori_loop(0, pairs, two_tiles, 0)
    lax.fori_loop(2 * pairs, qi, one_tile, 0)
    process(qi, 1, True)

    lp = lam_ref[...]
    lam = (jnp.exp(jnp.sum(lp[0:1] * lp[1:2], axis=-1, keepdims=True))
           - jnp.exp(jnp.sum(lp[2:3] * lp[3:4], axis=-1, keepdims=True)) + lam_init)
    a0, a1 = acc_ref[0], acc_ref[1]
    o = a0[:, :hw] / a0[:, hw:] - lam * (a1[:, :hw] / a1[:, hw:])
    ms = jnp.mean(o * o, axis=-1, keepdims=True)
    o = o * lax.rsqrt(ms + LN_EPS) * g_ref[...] * (1.0 - lam_init)
    o_ref[...] = o.astype(o_ref.dtype)


def _diff_attention(q, k, v, lam_qk, subln_g, batch, seq, lam_init):
    n = batch * seq
    tq = min(DIFF_TQ, seq)
    nq = seq // tq
    hw = 2 * HEAD_DIM
    slopes2 = np.asarray(_alibi_slopes(DIFF_HEADS), np.float64) * LOG2E
    qt, kt = _alibi_templates(tq)
    ones_col = np.ones((tq, hw), np.float32)
    rel = np.arange(tq)[None, :] - np.arange(tq)[:, None]
    dmask = jnp.asarray(np.where(rel <= 0, 0.0, NEG).astype(np.float32))
    kernel = functools.partial(_diff_kernel, tq=tq, lam_init=lam_init)
    fixed = lambda b, h, i: (0, 0)
    per_head = lambda b, h, i: (h, 0, 0, 0)
    return pl.pallas_call(
        kernel,
        grid=(batch, DIFF_HEADS, nq),
        in_specs=[
            pl.BlockSpec(memory_space=pltpu.SMEM),
            pl.BlockSpec((tq, hw), lambda b, h, i: (b * nq + i, h)),
            pl.BlockSpec((seq, hw), lambda b, h, i: (b, h)),
            pl.BlockSpec((seq, hw), lambda b, h, i: (b, h)),
            pl.BlockSpec((1, 2, tq, hw), per_head),
            pl.BlockSpec((1, 2, tq, hw), per_head),
            pl.BlockSpec((tq, hw), fixed),
            pl.BlockSpec((tq, tq), fixed),
            pl.BlockSpec(lam_qk.shape, fixed),
            pl.BlockSpec(subln_g.shape, fixed),
        ],
        out_specs=pl.BlockSpec((tq, hw), lambda b, h, i: (b * nq + i, h)),
        out_shape=jax.ShapeDtypeStruct((n, DIFF_HEADS * hw), MXU_DTYPE),
        scratch_shapes=[
            pltpu.VMEM((2, tq, hw), MXU_DTYPE),
            pltpu.VMEM((2, tq, 1), jnp.float32),
            pltpu.VMEM((2, tq, 2 * hw), jnp.float32),
        ],
        compiler_params=_params(("parallel", "parallel", "arbitrary")),
        name="diff_attention",
    )(jnp.asarray(slopes2, jnp.float32), q, k, v, qt, kt, jnp.asarray(ones_col, MXU_DTYPE), dmask,
      lam_qk, subln_g)


def _memkv_kernel(mem_ref, w_ref, o_ref):
    o_ref[0] = _dot(mem_ref[...].astype(MXU_DTYPE), w_ref[0].astype(MXU_DTYPE)).astype(o_ref.dtype)


def _mem_kv(mem2d, w_mem_kv):
    depth, d, w = w_mem_kv.shape
    rows = mem2d.shape[0]
    return pl.pallas_call(
        _memkv_kernel,
        grid=(depth,),
        in_specs=[pl.BlockSpec((rows, d), lambda l: (0, 0)),
                  pl.BlockSpec((1, d, w), lambda l: (l, 0, 0))],
        out_specs=pl.BlockSpec((1, rows, w), lambda l: (l, 0, 0)),
        out_shape=jax.ShapeDtypeStruct((depth, rows, w), MXU_DTYPE),
        compiler_params=_params(("parallel",)),
        name="mem_kv_proj",
    )(mem2d, w_mem_kv)


ROW_SLAB = 4
_HI16 = 0xFFFF0000


def _bf16_bits_hi(x):
    b = lax.bitcast_convert_type(x, jnp.uint32)
    b = b + jnp.uint32(0x7FFF) + ((b >> 16) & jnp.uint32(1))
    return b & jnp.uint32(_HI16)


def _pack_rows(z, ref, base, t):
    half = z.shape[1] // 2
    words = _bf16_bits_hi(z[:, half:]) | (_bf16_bits_hi(z[:, :half]) >> 16)
    for s in range(ROW_SLAB):
        ref[pl.ds(base * ROW_SLAB + s, t, stride=ROW_SLAB), :] = words[:, s * 128:(s + 1) * 128]


def _unpack_rows(ref, base, t):
    lo, hi = [], []
    for s in range(ROW_SLAB):
        w = ref[pl.ds(base * ROW_SLAB + s, t, stride=ROW_SLAB), :]
        lo.append(lax.bitcast_convert_type(w << 16, jnp.float32))
        hi.append(lax.bitcast_convert_type(w & jnp.uint32(_HI16), jnp.float32))
    return lo + hi


def _outproj_kernel(mix_ref, qm_ref, kv_ref, wo_ref, x_ref, g_ref, b_ref, o_ref, op_ref):
    qm = qm_ref[...]
    kvm = kv_ref[0]
    mw = MEM_HEADS * HEAD_DIM
    heads = [slice(h * HEAD_DIM, (h + 1) * HEAD_DIM) for h in range(MEM_HEADS)]
    scores = [_dot_nt(qm[:, hs], kvm[:, hs]) for hs in heads]
    tops = [jnp.max(s, axis=-1, keepdims=True) for s in scores]
    probs = [jnp.exp(s - m) for s, m in zip(scores, tops)]
    sums = [jnp.sum(p, axis=-1, keepdims=True) for p in probs]
    outs = [_dot(p.astype(MXU_DTYPE), kvm[:, mw + hs.start: mw + hs.stop]) / l
            for p, hs, l in zip(probs, heads, sums)]
    mem_out = jnp.concatenate(outs, axis=-1).astype(MXU_DTYPE)
    mixw = mix_ref.shape[1]
    y = _dot(mix_ref[...], wo_ref[0:mixw, :]) + _dot(mem_out, wo_ref[mixw:, :])
    z = DN_ALPHA * x_ref[...] + y
    zn = _layer_norm(z, g_ref[...], b_ref[...])
    o_ref[...] = zn
    _pack_rows(zn, op_ref, 0, zn.shape[0])


def _out_proj(mix, qm, kv_l, w_out, x, ln_g, ln_b, batch, seq):
    n, d = x.shape
    tm = min(PROJ_TM, seq)
    per_b = seq // tm
    mlen, kvw = kv_l.shape[1], kv_l.shape[2]
    row = lambda i: (i, 0)
    whole = lambda i: (0, 0)
    xn, xp = pl.pallas_call(
        _outproj_kernel,
        grid=(n // tm,),
        in_specs=[
            pl.BlockSpec((tm, mix.shape[1]), row),
            pl.BlockSpec((tm, qm.shape[1]), row),
            pl.BlockSpec((1, mlen, kvw), lambda i: (i // per_b, 0, 0)),
            pl.BlockSpec(w_out.shape, whole),
            pl.BlockSpec((tm, d), row),
            pl.BlockSpec((1, d), whole),
            pl.BlockSpec((1, d), whole),
        ],
        out_specs=[pl.BlockSpec((tm, d), row), pl.BlockSpec((tm * ROW_SLAB, 128), row)],
        out_shape=[jax.ShapeDtypeStruct((n, d), jnp.float32),
                   jax.ShapeDtypeStruct((n * ROW_SLAB, 128), jnp.uint32)],
        compiler_params=_params(("parallel",)),
        name="out_proj_ln",
    )(mix, qm, kv_l, w_out, x, ln_g, ln_b)
    return xn, xp.reshape(n, ROW_SLAB, 128)


def _split_hi_lo(a):
    hi = a.astype(jnp.bfloat16)
    lo = (a - hi.astype(jnp.float32)).astype(jnp.bfloat16)
    return hi, lo


def _route_kernel(x_ref, w_ref, bias_ref, eid_ref, tri_ref, gate_ref, rank_ref, cnt_ref, carry_ref):
    i = pl.program_id(0)

    @pl.when(i == 0)
    def _():
        carry_ref[...] = jnp.zeros(carry_ref.shape, jnp.float32)

    xh, xl = _split_hi_lo(x_ref[...])
    wh, wl = _split_hi_lo(w_ref[...])
    logits = _dot_nt(wh, xh) + _dot_nt(wh, xl) + _dot_nt(wl, xh)
    aff = 1.0 / (1.0 + jnp.exp(-logits))
    sel = aff + bias_ref[...]
    t = sel.shape[1]

    sel3 = sel.reshape(PER_GROUP, N_GROUPS, t)
    jidx = lax.broadcasted_iota(jnp.int32, (PER_GROUP, N_GROUPS, t), 0)
    m1 = jnp.max(sel3, axis=0, keepdims=True)
    first = jnp.min(jnp.where(sel3 == m1, jidx, PER_GROUP), axis=0, keepdims=True)
    m2 = jnp.max(jnp.where(jidx == first, -jnp.inf, sel3), axis=0, keepdims=True)
    gscore = (m1 + m2)[0]

    gid = lax.broadcasted_iota(jnp.int32, (N_GROUPS, t), 0)
    grank = jnp.zeros((N_GROUPS, t), jnp.int32)
    for g2 in range(N_GROUPS):
        other = gscore[g2:g2 + 1, :]
        beats = jnp.logical_or(other > gscore, jnp.logical_and(other == gscore, g2 < gid))
        grank = grank + beats.astype(jnp.int32)
    gmask = grank < TOPK_GROUPS
    emask = jnp.broadcast_to(gmask[None], (PER_GROUP, N_GROUPS, t)).reshape(N_EXPERTS, t)
    val = jnp.where(emask, sel, -jnp.inf)

    eid = eid_ref[...]
    chosen = jnp.zeros((N_EXPERTS, t), jnp.bool_)
    for _ in range(TOP_K):
        best = jnp.max(val, axis=0, keepdims=True)
        pick = jnp.min(jnp.where(val == best, eid, N_EXPERTS), axis=0, keepdims=True)
        hit = eid == pick
        chosen = jnp.logical_or(chosen, hit)
        val = jnp.where(hit, -jnp.inf, val)
    picked = jnp.where(chosen, aff, 0.0)
    gate_ref[...] = picked / jnp.sum(picked, axis=0, keepdims=True) * ROUTED_SCALE

    ind = chosen.astype(jnp.bfloat16)
    prefix = _dot(ind, tri_ref[...])
    carry = carry_ref[...]
    rank_ref[...] = jnp.where(chosen, prefix + carry, -1.0)
    carry = carry + jnp.sum(chosen.astype(jnp.float32), axis=1, keepdims=True)
    carry_ref[...] = carry
    cnt_ref[...] = carry


def _route(x, w_router_t, bias_col, eid_col):
    n, d = x.shape
    t = min(ROUTE_T, n)
    tri = jnp.asarray(np.triu(np.ones((t, t), np.float32), 1), jnp.bfloat16)
    whole = lambda i: (0, 0)
    return pl.pallas_call(
        _route_kernel,
        grid=(n // t,),
        in_specs=[
            pl.BlockSpec((t, d), lambda i: (i, 0)),
            pl.BlockSpec((N_EXPERTS, d), whole),
            pl.BlockSpec((N_EXPERTS, 1), whole),
            pl.BlockSpec((N_EXPERTS, 1), whole),
            pl.BlockSpec((t, t), whole),
        ],
        out_specs=[
            pl.BlockSpec((N_EXPERTS, t), lambda i: (0, i)),
            pl.BlockSpec((N_EXPERTS, t), lambda i: (0, i)),
            pl.BlockSpec((N_EXPERTS, 1), whole),
        ],
        out_shape=[
            jax.ShapeDtypeStruct((N_EXPERTS, n), jnp.float32),
            jax.ShapeDtypeStruct((N_EXPERTS, n), jnp.float32),
            jax.ShapeDtypeStruct((N_EXPERTS, 1), jnp.float32),
        ],
        scratch_shapes=[pltpu.VMEM((N_EXPERTS, 1), jnp.float32)],
        compiler_params=_params(("arbitrary",)),
        name="moe_route",
    )(x, w_router_t, bias_col, eid_col, tri)


def _dest_kernel(gate_ref, rank_ref, pstart_ref, ltri_ref, dest_ref, g_ref):
    rank = rank_ref[...]
    chosen = rank >= 0.0
    slot = _dot(ltri_ref[...], chosen.astype(jnp.bfloat16))
    row = rank + pstart_ref[...]
    gate = gate_ref[...]
    dests, gates = [], []
    for k in range(TOP_K):
        hit = jnp.logical_and(chosen, slot == float(k))
        dests.append(jnp.sum(jnp.where(hit, row, 0.0), axis=0, keepdims=True))
        gates.append(jnp.sum(jnp.where(hit, gate, 0.0), axis=0, keepdims=True))
    dest_ref[...] = jnp.concatenate(dests, axis=0).astype(jnp.int32)
    g_ref[...] = jnp.concatenate(gates, axis=0)


def _dest(gate_dense, rank_dense, pstart_col):
    e, n = gate_dense.shape
    t = min(ROUTE_T, n)
    ltri = jnp.asarray(np.tril(np.ones((e, e), np.float32), -1), jnp.bfloat16)
    whole = lambda i: (0, 0)
    tile = lambda i: (0, i)
    return pl.pallas_call(
        _dest_kernel,
        grid=(n // t,),
        in_specs=[pl.BlockSpec((e, t), tile), pl.BlockSpec((e, t), tile),
                  pl.BlockSpec((e, 1), whole), pl.BlockSpec((e, e), whole)],
        out_specs=[pl.BlockSpec((TOP_K, t), tile), pl.BlockSpec((TOP_K, t), tile)],
        out_shape=[jax.ShapeDtypeStruct((TOP_K, n), jnp.int32),
                   jax.ShapeDtypeStruct((TOP_K, n), jnp.float32)],
        compiler_params=_params(("parallel",)),
        name="moe_dest",
    )(gate_dense, rank_dense, pstart_col, ltri)


def _dispatch_kernel(dest_ref, padlo_ref, padn_ref, nu_ref, xp_ref, xs_ref, zero_ref, sem, zsem, *,
                     t_tile, nblk):
    i = pl.program_id(0)

    @pl.when(i == 0)
    def _():
        zero_ref[...] = jnp.zeros(zero_ref.shape, zero_ref.dtype)

        def for_each_pad_copy(e, fn):
            off = padlo_ref[e]
            cnt = padn_ref[e]
            for bit in reversed(range(MOE_BLOCK.bit_length() - 1)):
                size = 1 << bit
                on = (cnt & size) != 0

                @pl.when(on)
                def _():
                    fn(pltpu.make_async_copy(zero_ref.at[pl.ds(0, size)], xs_ref.at[pl.ds(off, size)], zsem))
                off = off + jnp.where(on, size, 0)

        def tail_copy(b):
            return pltpu.make_async_copy(zero_ref, xs_ref.at[pl.ds(b * MOE_BLOCK, MOE_BLOCK)], zsem)

        def start_pad(e, c):
            for_each_pad_copy(e, lambda cp: cp.start())
            return c

        def wait_pad(e, c):
            for_each_pad_copy(e, lambda cp: cp.wait())
            return c

        def start_tail(b, c):
            tail_copy(b).start()
            return c

        def wait_tail(b, c):
            tail_copy(b).wait()
            return c

        lax.fori_loop(0, N_EXPERTS, start_pad, 0)
        lax.fori_loop(nu_ref[0], nblk, start_tail, 0)
        lax.fori_loop(0, N_EXPERTS, wait_pad, 0)
        lax.fori_loop(nu_ref[0], nblk, wait_tail, 0)

    for t in range(t_tile):
        for k in range(TOP_K):
            pltpu.make_async_copy(xp_ref.at[t], xs_ref.at[dest_ref[k, t]], sem).start(priority=k % 2)
    for k in range(TOP_K):
        pltpu.make_async_copy(xp_ref, xs_ref.at[pl.ds(0, t_tile)], sem).wait()


def _dispatch(xp, dest, pad_lo, pad_n, n_used, n_rows):
    n = xp.shape[0]
    t = min(DISPATCH_T, n)
    nblk = n_rows // MOE_BLOCK
    smem = pl.BlockSpec(memory_space=pltpu.SMEM)
    return pl.pallas_call(
        functools.partial(_dispatch_kernel, t_tile=t, nblk=nblk),
        grid=(n // t,),
        in_specs=[
            pl.BlockSpec((TOP_K, t), lambda i: (0, i), memory_space=pltpu.SMEM),
            smem, smem, smem,
            pl.BlockSpec((t, ROW_SLAB, 128), lambda i: (i, 0, 0)),
        ],
        out_specs=pl.BlockSpec(memory_space=pl.ANY),
        out_shape=jax.ShapeDtypeStruct((n_rows, ROW_SLAB, 128), jnp.uint32),
        scratch_shapes=[pltpu.VMEM((MOE_BLOCK, ROW_SLAB, 128), jnp.uint32),
                        pltpu.SemaphoreType.DMA(()), pltpu.SemaphoreType.DMA(())],
        compiler_params=_params(("arbitrary",)),
        name="moe_dispatch",
    )(dest, pad_lo, pad_n, n_used, xp)


def _experts_kernel(be_ref, nu_ref, xs_ref, wg_ref, wu_ref, wd_ref, ys_ref, wgc_ref, wuc_ref, wdc_ref):
    b = pl.program_id(0)
    new_expert = jnp.logical_or(b == 0, be_ref[b] != be_ref[jnp.maximum(b - 1, 0)])

    @pl.when(jnp.logical_and(b < nu_ref[0], new_expert))
    def _():
        wgc_ref[...] = wg_ref[0, 0].astype(MXU_DTYPE)
        wuc_ref[...] = wu_ref[0, 0].astype(MXU_DTYPE)
        wdc_ref[...] = wd_ref[0, 0].astype(MXU_DTYPE)

    @pl.when(b < nu_ref[0])
    def _():
        xb = jnp.concatenate(_unpack_rows(xs_ref, 0, MOE_BLOCK), axis=1).astype(MXU_DTYPE)
        g = _dot(xb, wgc_ref[...])
        u = _dot(xb, wuc_ref[...])
        hmid = (_silu(g) * u).astype(MXU_DTYPE)
        _pack_rows(_dot(hmid, wdc_ref[...]), ys_ref, 0, MOE_BLOCK)

    @pl.when(b >= nu_ref[0])
    def _():
        ys_ref[...] = jnp.zeros(ys_ref.shape, ys_ref.dtype)


def _experts(xs2d, block_e, n_used, w_gate, w_up, w_down, layer):
    d, hid = w_gate.shape[2], w_gate.shape[3]
    blk_rows = MOE_BLOCK * ROW_SLAB
    nblk = xs2d.shape[0] // blk_rows
    blk = lambda b, be, nu: (b, 0)
    used_blk = lambda b, be, nu: (jnp.minimum(b, nu[0] - 1), 0)
    expert = lambda b, be, nu: (layer, be[b], 0, 0)
    grid_spec = pltpu.PrefetchScalarGridSpec(
        num_scalar_prefetch=2,
        grid=(nblk,),
        in_specs=[
            pl.BlockSpec((blk_rows, 128), used_blk),
            pl.BlockSpec((1, 1, d, hid), expert),
            pl.BlockSpec((1, 1, d, hid), expert),
            pl.BlockSpec((1, 1, hid, d), expert),
        ],
        out_specs=pl.BlockSpec((blk_rows, 128), blk),
        scratch_shapes=[pltpu.VMEM((d, hid), MXU_DTYPE), pltpu.VMEM((d, hid), MXU_DTYPE),
                        pltpu.VMEM((hid, d), MXU_DTYPE)],
    )
    return pl.pallas_call(
        _experts_kernel,
        grid_spec=grid_spec,
        out_shape=jax.ShapeDtypeStruct(xs2d.shape, jnp.uint32),
        compiler_params=_params(("arbitrary",)),
        name="moe_experts",
    )(block_e, n_used, xs2d, w_gate, w_up, w_down)


def _combine_kernel(dest_ref, dnext_ref, g_ref, x_ref, wsg_ref, wsu_ref, wsd_ref, lg_ref, lb_ref, ys_ref,
                    ys2d_ref, o_ref, buf_ref, sem, *, t_tile):
    i = pl.program_id(0)
    last = pl.num_programs(0) - 1
    slot_rows = TOP_K * t_tile
    slot = lax.rem(i, 2)

    def gather(dref, s, unrolled):
        base = s * slot_rows

        def issue(t, carry):
            for k in range(TOP_K):
                r = pl.multiple_of((base + k * t_tile + t) * ROW_SLAB, ROW_SLAB)
                pltpu.make_async_copy(ys_ref.at[dref[k, t]], buf_ref.at[pl.ds(r, ROW_SLAB), :],
                                      sem).start(priority=k % 2)
            return carry

        if unrolled:
            for t in range(t_tile):
                issue(t, 0)
        else:
            lax.fori_loop(0, t_tile, issue, 0)

    def wait(s):
        n = t_tile * ROW_SLAB
        for k in range(TOP_K):
            r = pl.multiple_of((s * slot_rows + k * t_tile) * ROW_SLAB, ROW_SLAB)
            pltpu.make_async_copy(ys2d_ref.at[pl.ds(0, n), :], buf_ref.at[pl.ds(r, n), :], sem).wait()

    @pl.when(i == 0)
    def _():
        gather(dest_ref, 0, False)

    wait(slot)
    gather(dnext_ref, 1 - slot, True)

    x = x_ref[...]
    xb = x.astype(MXU_DTYPE)
    hmid = (_silu(_dot(xb, wsg_ref[...])) * _dot(xb, wsu_ref[...])).astype(MXU_DTYPE)
    z = DN_ALPHA * x + _dot(hmid, wsd_ref[...])

    gt = jnp.transpose(g_ref[...])
    cols = None
    for k in range(TOP_K):
        tiles = _unpack_rows(buf_ref, slot * slot_rows + k * t_tile, t_tile)
        gk = gt[:, k:k + 1]
        cols = [tl * gk for tl in tiles] if cols is None else [c + tl * gk for c, tl in zip(cols, tiles)]
    z = z + jnp.concatenate(cols, axis=1)
    o_ref[...] = _layer_norm(z, lg_ref[...], lb_ref[...])

    @pl.when(i == last)
    def _():
        wait(1 - slot)


def _combine(x, dest, gates, ys, ws_gate, ws_up, ws_down, ln_g, ln_b):
    n, d = x.shape
    t = min(COMBINE_T, n)
    steps = n // t
    whole = lambda i: (0, 0)
    tile = lambda i: (0, i)
    row = lambda i: (i, 0)
    nxt = lambda i: (0, jnp.minimum(i + 1, steps - 1))
    return pl.pallas_call(
        functools.partial(_combine_kernel, t_tile=t),
        grid=(steps,),
        in_specs=[
            pl.BlockSpec((TOP_K, t), tile, memory_space=pltpu.SMEM),
            pl.BlockSpec((TOP_K, t), nxt, memory_space=pltpu.SMEM),
            pl.BlockSpec((TOP_K, t), tile),
            pl.BlockSpec((t, d), row),
            pl.BlockSpec(ws_gate.shape, whole),
            pl.BlockSpec(ws_up.shape, whole),
            pl.BlockSpec(ws_down.shape, whole),
            pl.BlockSpec((1, d), whole),
            pl.BlockSpec((1, d), whole),
            pl.BlockSpec(memory_space=pl.ANY),
            pl.BlockSpec(memory_space=pl.ANY),
        ],
        out_specs=pl.BlockSpec((t, d), row),
        out_shape=jax.ShapeDtypeStruct((n, d), jnp.float32),
        scratch_shapes=[pltpu.VMEM((2 * TOP_K * t * ROW_SLAB, 128), jnp.uint32), pltpu.SemaphoreType.DMA(())],
        compiler_params=_params(("arbitrary",)),
        name="moe_combine",
    )(dest, dest, gates, x, ws_gate, ws_up, ws_down, ln_g, ln_b, ys, ys.reshape(-1, 128))


def _moe(x, xp, w_router_t, bias_col, eid_col, inv_perm, w_gate, w_up, w_down, layer, ws_gate, ws_up, ws_down,
         ln_g, ln_b):
    n, d = x.shape
    gate_dense, rank_dense, counts = _route(x, w_router_t, bias_col, eid_col)
    cnt = counts[:, 0].astype(jnp.int32)[inv_perm]
    padded = (cnt + MOE_BLOCK - 1) // MOE_BLOCK * MOE_BLOCK
    pends = jnp.cumsum(padded)
    pstarts = pends - padded
    nk = n * TOP_K
    n_rows = (nk + N_EXPERTS * (MOE_BLOCK - 1) + MOE_BLOCK - 1) // MOE_BLOCK * MOE_BLOCK
    nblk = n_rows // MOE_BLOCK
    blk_start = jnp.arange(nblk, dtype=jnp.int32) * MOE_BLOCK
    block_e = jnp.minimum(jnp.sum(blk_start[:, None] >= pends[None, :], axis=1), N_EXPERTS - 1).astype(jnp.int32)
    n_used = (pends[-1:] // MOE_BLOCK).astype(jnp.int32)
    perm = jnp.argsort(inv_perm)
    pstart_col = pstarts[perm].astype(jnp.float32)[:, None]
    dest, gates = _dest(gate_dense, rank_dense, pstart_col)
    xs = _dispatch(xp, dest, (pstarts + cnt).astype(jnp.int32), (padded - cnt).astype(jnp.int32), n_used, n_rows)
    ys = _experts(xs.reshape(n_rows * ROW_SLAB, 128), block_e, n_used, w_gate, w_up, w_down, layer)
    return _combine(x, dest, gates, ys.reshape(n_rows, ROW_SLAB, 128), ws_gate, ws_up, ws_down, ln_g, ln_b)


def kernel(x, mem, w_in_a, sinks, w_in_b, lambda_qk, subln_g, w_mem_kv, w_out, ln_g, ln_b, w_router,
           router_bias, w_gate, w_up, w_down, w_shared_gate, w_shared_up, w_shared_down):
    batch, seq, d = x.shape
    n = batch * seq
    cdt = MXU_DTYPE
    qa, kva = SWA_Q_HEADS * HEAD_DIM, SWA_KV_HEADS * HEAD_DIM
    qb, vb = DIFF_HEADS * 2 * HEAD_DIM, DIFF_HEADS * DIFF_V_DIM

    rows = np.arange(N_EXPERTS)
    perm = (rows % N_GROUPS) * PER_GROUP + rows // N_GROUPS
    inv_perm = jnp.asarray(np.argsort(perm), jnp.int32)
    eid_col = jnp.asarray(perm[:, None], jnp.int32)

    xf = x.reshape(n, d)
    kv_all = _mem_kv(mem.reshape(-1, d), w_mem_kv)
    mlen = mem.shape[1]
    kv_all = kv_all.reshape(DEPTH * batch, mlen, kv_all.shape[-1])

    def in_proj_of(i):
        j = i // 2
        if i % 2 == 0:
            w = w_in_a[j].astype(cdt)
            return ([w[:, :qa], w[:, qa:qa + kva], w[:, qa + kva:qa + 2 * kva], w[:, qa + 2 * kva:]],
                    (QK_SCALE, 1.0, 1.0, QK_SCALE))
        w = w_in_b[j].astype(cdt)
        return ([w[:, :qb], w[:, qb:2 * qb], w[:, 2 * qb:2 * qb + vb], w[:, 2 * qb + vb:]],
                (QK_SCALE * LOG2E, 1.0, 1.0, QK_SCALE))

    for i in range(DEPTH):
        j = i // 2
        q, k, v, qm = _project(xf, *in_proj_of(i))
        if i % 2 == 0:
            mix = _swa_attention(q, k, v, sinks[j], batch, seq)
        else:
            lam_init = 0.8 - 0.6 * math.exp(-0.3 * i)
            mix = _diff_attention(q, k, v, lambda_qk[j], subln_g[j][None, :], batch, seq, lam_init)
        kv_l = kv_all[i * batch:(i + 1) * batch]
        xf, xp = _out_proj(mix, qm, kv_l, w_out[i].astype(cdt), xf, ln_g[i, 0][None, :], ln_b[i, 0][None, :],
                           batch, seq)
        w_router_t = jnp.transpose(w_router[i])[perm]
        bias_col = router_bias[i][perm][:, None]
        xf = _moe(xf, xp, w_router_t, bias_col, eid_col, inv_perm,
                  w_gate, w_up, w_down, i,
                  w_shared_gate[i].astype(cdt), w_shared_up[i].astype(cdt), w_shared_down[i].astype(cdt),
                  ln_g[i, 1][None, :], ln_b[i, 1][None, :])
    return xf.reshape(batch, seq, d)
```

```python
import functools
import math

import jax
import jax.numpy as jnp
import numpy as np
from jax import lax
from jax.experimental import pallas as pl
from jax.experimental.pallas import tpu as pltpu

HEAD_DIM = 64
MEM_HEADS = 4
SWA_Q_HEADS = 12
SWA_KV_HEADS = 3
SWA_GROUP = SWA_Q_HEADS // SWA_KV_HEADS
WINDOW = 128
DIFF_HEADS = 6
DIFF_V_DIM = 2 * HEAD_DIM
N_EXPERTS = 64
TOP_K = 8
N_GROUPS = 8
PER_GROUP = N_EXPERTS // N_GROUPS
TOPK_GROUPS = 4
ROUTED_SCALE = 2.5
MOE_BLOCK = 512
DEPTH = 4
DN_ALPHA = (2.0 * DEPTH) ** 0.25
LN_EPS = 1e-5
NEG = -1e30
QK_SCALE = HEAD_DIM ** -0.5

MXU_DTYPE = jnp.bfloat16

PROJ_TM = 512
DIFF_TQ = 1024
ROUTE_T = 512
DISPATCH_T = 256
COMBINE_T = 256
VMEM_LIMIT = 56 * 1024 * 1024


def _alibi_slopes(n):
    def pow2_slopes(m):
        start = 2.0 ** (-8.0 / m)
        return [start ** (i + 1) for i in range(m)]
    if (n & (n - 1)) == 0:
        return pow2_slopes(n)
    c = 2 ** int(math.floor(math.log2(n)))
    return pow2_slopes(c) + pow2_slopes(2 * c)[0::2][: n - c]


def _dot(a, b):
    return jnp.dot(a, b, preferred_element_type=jnp.float32)


def _dot_nt(a, b):
    return lax.dot_general(a, b, (((1,), (1,)), ((), ())), preferred_element_type=jnp.float32)


def _params(sem):
    return pltpu.CompilerParams(dimension_semantics=sem, vmem_limit_bytes=VMEM_LIMIT)


def _layer_norm(z, g, b):
    mu = jnp.mean(z, axis=-1, keepdims=True)
    zc = z - mu
    var = jnp.mean(zc * zc, axis=-1, keepdims=True)
    return zc * lax.rsqrt(var + LN_EPS) * g + b


def _silu(x):
    return x / (1.0 + jnp.exp(-x))


def _proj_kernel(*refs, scales):
    nw = len(scales)
    x_ref = refs[0]
    w_refs = refs[1:1 + nw]
    o_refs = refs[1 + nw:]
    xb = x_ref[...].astype(MXU_DTYPE)
    for w_ref, o_ref, sc in zip(w_refs, o_refs, scales):
        acc = _dot(xb, w_ref[...])
        if sc != 1.0:
            acc = acc * sc
        o_ref[...] = acc.astype(o_ref.dtype)


def _project(x, weights, scales):
    n, d = x.shape
    tm = min(PROJ_TM, n)
    in_specs = [pl.BlockSpec((tm, d), lambda i: (i, 0))]
    in_specs += [pl.BlockSpec(w.shape, lambda i: (0, 0)) for w in weights]
    out_specs = [pl.BlockSpec((tm, w.shape[1]), lambda i: (i, 0)) for w in weights]
    out_shape = [jax.ShapeDtypeStruct((n, w.shape[1]), MXU_DTYPE) for w in weights]
    return pl.pallas_call(
        functools.partial(_proj_kernel, scales=tuple(scales)),
        grid=(n // tm,),
        in_specs=in_specs,
        out_specs=out_specs,
        out_shape=out_shape,
        compiler_params=_params(("parallel",)),
        name="in_proj",
    )(x, *weights)


def _swa_kernel(sinks_ref, bias_ref, q_ref, kp_ref, kc_ref, vp_ref, vc_ref, o_ref):
    i = pl.program_id(1)
    q = q_ref[...]
    kp, kc, vp, vc = kp_ref[...], kc_ref[...], vp_ref[...], vc_ref[...]
    rows = SWA_GROUP * WINDOW
    head_of_row = lax.broadcasted_iota(jnp.int32, (rows, 1), 0) // WINDOW
    vpad = jnp.concatenate([jnp.zeros((WINDOW, HEAD_DIM), MXU_DTYPE),
                            jnp.ones((WINDOW, 2 * HEAD_DIM), MXU_DTYPE)], axis=1)
    outs = []
    for kv in range(SWA_KV_HEADS):
        ks = slice(kv * HEAD_DIM, (kv + 1) * HEAD_DIM)
        kph, kch, vph, vch = kp[:, ks], kc[:, ks], vp[:, ks], vc[:, ks]
        h0 = kv * SWA_GROUP
        qg = jnp.concatenate([q[:, (h0 + g) * HEAD_DIM:(h0 + g + 1) * HEAD_DIM] for g in range(SWA_GROUP)],
                             axis=0)
        sink = jnp.zeros((rows, 1), jnp.float32) + sinks_ref[h0]
        for g in range(1, SWA_GROUP):
            sink = jnp.where(head_of_row == g, sinks_ref[h0 + g], sink)
        sp = jnp.where(i > 0, _dot_nt(qg, kph) + bias_ref[kv, 0], NEG)
        sc = _dot_nt(qg, kch) + bias_ref[kv, 1]
        m = jnp.maximum(jnp.maximum(jnp.max(sp, axis=-1, keepdims=True),
                                    jnp.max(sc, axis=-1, keepdims=True)), sink)
        pp = jnp.exp(sp - m).astype(MXU_DTYPE)
        pc = jnp.exp(sc - m).astype(MXU_DTYPE)
        ov = _dot(pp, jnp.concatenate([vph, vpad], axis=1)) + _dot(pc, jnp.concatenate([vch, vpad], axis=1))
        denom = ov[:, 2 * HEAD_DIM:3 * HEAD_DIM] + jnp.exp(sink - m)
        o = ov[:, :HEAD_DIM] / denom
        outs += [o[g * WINDOW:(g + 1) * WINDOW] for g in range(SWA_GROUP)]
    o_ref[...] = jnp.concatenate(outs, axis=-1).astype(o_ref.dtype)


def _swa_bias():
    slopes = np.asarray(_alibi_slopes(SWA_Q_HEADS), np.float64).reshape(SWA_KV_HEADS, SWA_GROUP)
    qi = np.arange(WINDOW)[:, None]
    kj = np.arange(WINDOW)[None, :]
    dist_c = qi - kj
    dist_p = dist_c + WINDOW
    out = np.empty((SWA_KV_HEADS, 2, SWA_GROUP, WINDOW, WINDOW), np.float32)
    for kv in range(SWA_KV_HEADS):
        for g in range(SWA_GROUP):
            out[kv, 0, g] = np.where(dist_p < WINDOW, -slopes[kv, g] * dist_p, NEG)
            out[kv, 1, g] = np.where(dist_c >= 0, -slopes[kv, g] * dist_c, NEG)
    return jnp.asarray(out.reshape(SWA_KV_HEADS, 2, SWA_GROUP * WINDOW, WINDOW))


def _swa_attention(q, k, v, sinks, batch, seq):
    n = batch * seq
    nb = seq // WINDOW
    qw, kw = q.shape[1], k.shape[1]
    cur = lambda b, i: (b * nb + i, 0)
    prev = lambda b, i: (b * nb + jnp.maximum(i - 1, 0), 0)
    bias = _swa_bias()
    return pl.pallas_call(
        _swa_kernel,
        grid=(batch, nb),
        in_specs=[
            pl.BlockSpec(memory_space=pltpu.SMEM),
            pl.BlockSpec(bias.shape, lambda b, i: (0, 0, 0, 0)),
            pl.BlockSpec((WINDOW, qw), cur),
            pl.BlockSpec((WINDOW, kw), prev),
            pl.BlockSpec((WINDOW, kw), cur),
            pl.BlockSpec((WINDOW, kw), prev),
            pl.BlockSpec((WINDOW, kw), cur),
        ],
        out_specs=pl.BlockSpec((WINDOW, qw), cur),
        out_shape=jax.ShapeDtypeStruct((n, qw), MXU_DTYPE),
        compiler_params=_params(("parallel", "parallel")),
        name="swa_attention",
    )(sinks, bias, q, k, k, v, v)


LOG2E = math.log2(math.e)
_BIAS_PIECES = 3


def _bf16_pieces(c):
    out, rest = [], np.float64(c)
    for _ in range(_BIAS_PIECES):
        piece = np.asarray(rest, np.float32).astype(jnp.bfloat16).astype(np.float64)
        out.append(float(piece))
        rest = rest - piece
    return out


def _alibi_templates(tq):
    idx = np.arange(tq)
    hi, lo = (idx // 256) * 256, idx % 256
    hw = 2 * HEAD_DIM
    qt = np.zeros((DIFF_HEADS, 2, tq, hw), np.float32)
    kt = np.zeros((DIFF_HEADS, 2, tq, hw), np.float32)
    for h, slope in enumerate(_alibi_slopes(DIFF_HEADS)):
        for c in range(2):
            base = HEAD_DIM if c == 0 else 0
            for p, piece in enumerate(_bf16_pieces(slope * LOG2E)):
                qt[h, c, :, base + p] = -hi
                kt[h, c, :, base + p] = piece
                qt[h, c, :, base + _BIAS_PIECES + p] = -lo
                kt[h, c, :, base + _BIAS_PIECES + p] = piece
                qt[h, c, :, base + 2 * _BIAS_PIECES + p] = piece
                kt[h, c, :, base + 2 * _BIAS_PIECES + p] = hi
                qt[h, c, :, base + 3 * _BIAS_PIECES + p] = piece
                kt[h, c, :, base + 3 * _BIAS_PIECES + p] = lo
    return jnp.asarray(qt, jnp.bfloat16), jnp.asarray(kt, jnp.bfloat16)


def _diff_kernel(slopes_ref, q_ref, k_ref, v_ref, qt_ref, kt_ref, ones_ref, dmask_ref, lam_ref, g_ref,
                 o_ref, qa_ref, m_ref, acc_ref, *, tq, lam_init):
    h = pl.program_id(1)
    qi = pl.program_id(2)
    slope2 = slopes_ref[h]
    hw = 2 * HEAD_DIM
    lane = lax.broadcasted_iota(jnp.int32, (1, hw), 1)
    first = lane < HEAD_DIM
    q = q_ref[...]
    qa_ref[0] = jnp.where(first, q, qt_ref[0, 0])
    qa_ref[1] = jnp.where(first, qt_ref[0, 1], q)
    m_ref[...] = jnp.full(m_ref.shape, NEG, jnp.float32)
    acc_ref[...] = jnp.zeros(acc_ref.shape, jnp.float32)
    ones = ones_ref[...]

    def softmax_pv(c, u, vaug, shift):
        m_old = m_ref[c]
        m_new = jnp.maximum(m_old, jnp.max(u, axis=-1, keepdims=True) + shift)
        p = jnp.exp2(u - (m_new - shift)).astype(MXU_DTYPE)
        alpha = jnp.exp2(m_old - m_new)
        acc_ref[c] = alpha * acc_ref[c] + _dot(p, vaug)
        m_ref[c] = m_new

    def process(ki, tiles, diagonal):
        vaugs, us, shifts = [], [], []
        for s in range(tiles):
            start = pl.multiple_of((ki + s) * tq, tq)
            k = k_ref[pl.ds(start, tq), :]
            v = v_ref[pl.ds(start, tq), :]
            vaugs.append(jnp.concatenate([v, ones], axis=1))
            kas = (jnp.where(first, k, kt_ref[0, 0]), jnp.where(first, kt_ref[0, 1], k))
            us.append([_dot_nt(qa_ref[c], kas[c]) for c in range(2)])
            shifts.append(slope2 * ((ki + s - qi) * tq).astype(jnp.float32))
        if diagonal:
            us = [[u + dmask_ref[...] for u in pair] for pair in us]
        for s in range(tiles):
            for c in range(2):
                softmax_pv(c, us[s][c], vaugs[s], shifts[s])

    def two_tiles(it, carry):
        process(2 * it, 2, False)
        return carry

    def one_tile(ki, carry):
        process(ki, 1, False)
        return carry

    pairs = qi // 2
    lax.fori_loop(0, pairs, two_tiles, 0)
    lax.fori_loop(2 * pairs, qi, one_tile, 0)
    process(qi, 1, True)

    lp = lam_ref[...]
    lam = (jnp.exp(jnp.sum(lp[0:1] * lp[1:2], axis=-1, keepdims=True))
           - jnp.exp(jnp.sum(lp[2:3] * lp[3:4], axis=-1, keepdims=True)) + lam_init)
    a0, a1 = acc_ref[0], acc_ref[1]
    o = a0[:, :hw] / a0[:, hw:] - lam * (a1[:, :hw] / a1[:, hw:])
    ms = jnp.mean(o * o, axis=-1, keepdims=True)
    o = o * lax.rsqrt(ms + LN_EPS) * g_ref[...] * (1.0 - lam_init)
    o_ref[...] = o.astype(o_ref.dtype)


def _diff_attention(q, k, v, lam_qk, subln_g, batch, seq, lam_init):
    n = batch * seq
    tq = min(DIFF_TQ, seq)
    nq = seq // tq
    hw = 2 * HEAD_DIM
    slopes2 = np.asarray(_alibi_slopes(DIFF_HEADS), np.float64) * LOG2E
    qt, kt = _alibi_templates(tq)
    ones_col = np.ones((tq, hw), np.float32)
    rel = np.arange(tq)[None, :] - np.arange(tq)[:, None]
    dmask = jnp.asarray(np.where(rel <= 0, 0.0, NEG).astype(np.float32))
    kernel = functools.partial(_diff_kernel, tq=tq, lam_init=lam_init)
    fixed = lambda b, h, i: (0, 0)
    per_head = lambda b, h, i: (h, 0, 0, 0)
    return pl.pallas_call(
        kernel,
        grid=(batch, DIFF_HEADS, nq),
        in_specs=[
            pl.BlockSpec(memory_space=pltpu.SMEM),
            pl.BlockSpec((tq, hw), lambda b, h, i: (b * nq + i, h)),
            pl.BlockSpec((seq, hw), lambda b, h, i: (b, h)),
            pl.BlockSpec((seq, hw), lambda b, h, i: (b, h)),
            pl.BlockSpec((1, 2, tq, hw), per_head),
            pl.BlockSpec((1, 2, tq, hw), per_head),
            pl.BlockSpec((tq, hw), fixed),
            pl.BlockSpec((tq, tq), fixed),
            pl.BlockSpec(lam_qk.shape, fixed),
            pl.BlockSpec(subln_g.shape, fixed),
        ],
        out_specs=pl.BlockSpec((tq, hw), lambda b, h, i: (b * nq + i, h)),
        out_shape=jax.ShapeDtypeStruct((n, DIFF_HEADS * hw), MXU_DTYPE),
        scratch_shapes=[
            pltpu.VMEM((2, tq, hw), MXU_DTYPE),
            pltpu.VMEM((2, tq, 1), jnp.float32),
            pltpu.VMEM((2, tq, 2 * hw), jnp.float32),
        ],
        compiler_params=_params(("parallel", "parallel", "arbitrary")),
        name="diff_attention",
    )(jnp.asarray(slopes2, jnp.float32), q, k, v, qt, kt, jnp.asarray(ones_col, MXU_DTYPE), dmask,
      lam_qk, subln_g)


def _memkv_kernel(mem_ref, w_ref, o_ref):
    o_ref[0] = _dot(mem_ref[...].astype(MXU_DTYPE), w_ref[0].astype(MXU_DTYPE)).astype(o_ref.dtype)


def _mem_kv(mem2d, w_mem_kv):
    depth, d, w = w_mem_kv.shape
    rows = mem2d.shape[0]
    return pl.pallas_call(
        _memkv_kernel,
        grid=(depth,),
        in_specs=[pl.BlockSpec((rows, d), lambda l: (0, 0)),
                  pl.BlockSpec((1, d, w), lambda l: (l, 0, 0))],
        out_specs=pl.BlockSpec((1, rows, w), lambda l: (l, 0, 0)),
        out_shape=jax.ShapeDtypeStruct((depth, rows, w), MXU_DTYPE),
        compiler_params=_params(("parallel",)),
        name="mem_kv_proj",
    )(mem2d, w_mem_kv)


ROW_SLAB = 4
_HI16 = 0xFFFF0000


def _bf16_bits_hi(x):
    b = lax.bitcast_convert_type(x, jnp.uint32)
    b = b + jnp.uint32(0x7FFF) + ((b >> 16) & jnp.uint32(1))
    return b & jnp.uint32(_HI16)


def _pack_rows(z, ref, base, t):
    half = z.shape[1] // 2
    words = _bf16_bits_hi(z[:, half:]) | (_bf16_bits_hi(z[:, :half]) >> 16)
    for s in range(ROW_SLAB):
        ref[pl.ds(base * ROW_SLAB + s, t, stride=ROW_SLAB), :] = words[:, s * 128:(s + 1) * 128]


def _unpack_rows(ref, base, t):
    lo, hi = [], []
    for s in range(ROW_SLAB):
        w = ref[pl.ds(base * ROW_SLAB + s, t, stride=ROW_SLAB), :]
        lo.append(lax.bitcast_convert_type(w << 16, jnp.float32))
        hi.append(lax.bitcast_convert_type(w & jnp.uint32(_HI16), jnp.float32))
    return lo + hi


def _outproj_kernel(mix_ref, qm_ref, kv_ref, wo_ref, x_ref, g_ref, b_ref, o_ref, op_ref):
    qm = qm_ref[...]
    kvm = kv_ref[0]
    mw = MEM_HEADS * HEAD_DIM
    heads = [slice(h * HEAD_DIM, (h + 1) * HEAD_DIM) for h in range(MEM_HEADS)]
    scores = [_dot_nt(qm[:, hs], kvm[:, hs]) for hs in heads]
    tops = [jnp.max(s, axis=-1, keepdims=True) for s in scores]
    probs = [jnp.exp(s - m) for s, m in zip(scores, tops)]
    sums = [jnp.sum(p, axis=-1, keepdims=True) for p in probs]
    outs = [_dot(p.astype(MXU_DTYPE), kvm[:, mw + hs.start: mw + hs.stop]) / l
            for p, hs, l in zip(probs, heads, sums)]
    mem_out = jnp.concatenate(outs, axis=-1).astype(MXU_DTYPE)
    mixw = mix_ref.shape[1]
    y = _dot(mix_ref[...], wo_ref[0:mixw, :]) + _dot(mem_out, wo_ref[mixw:, :])
    z = DN_ALPHA * x_ref[...] + y
    zn = _layer_norm(z, g_ref[...], b_ref[...])
    o_ref[...] = zn
    _pack_rows(zn, op_ref, 0, zn.shape[0])


def _out_proj(mix, qm, kv_l, w_out, x, ln_g, ln_b, batch, seq):
    n, d = x.shape
    tm = min(PROJ_TM, seq)
    per_b = seq // tm
    mlen, kvw = kv_l.shape[1], kv_l.shape[2]
    row = lambda i: (i, 0)
    whole = lambda i: (0, 0)
    xn, xp = pl.pallas_call(
        _outproj_kernel,
        grid=(n // tm,),
        in_specs=[
            pl.BlockSpec((tm, mix.shape[1]), row),
            pl.BlockSpec((tm, qm.shape[1]), row),
            pl.BlockSpec((1, mlen, kvw), lambda i: (i // per_b, 0, 0)),
            pl.BlockSpec(w_out.shape, whole),
            pl.BlockSpec((tm, d), row),
            pl.BlockSpec((1, d), whole),
            pl.BlockSpec((1, d), whole),
        ],
        out_specs=[pl.BlockSpec((tm, d), row), pl.BlockSpec((tm * ROW_SLAB, 128), row)],
        out_shape=[jax.ShapeDtypeStruct((n, d), jnp.float32),
                   jax.ShapeDtypeStruct((n * ROW_SLAB, 128), jnp.uint32)],
        compiler_params=_params(("parallel",)),
        name="out_proj_ln",
    )(mix, qm, kv_l, w_out, x, ln_g, ln_b)
    return xn, xp.reshape(n, ROW_SLAB, 128)


def _split_hi_lo(a):
    hi = a.astype(jnp.bfloat16)
    lo = (a - hi.astype(jnp.float32)).astype(jnp.bfloat16)
    return hi, lo


def _route_kernel(x_ref, w_ref, bias_ref, eid_ref, tri_ref, gate_ref, rank_ref, cnt_ref, carry_ref):
    i = pl.program_id(0)

    @pl.when(i == 0)
    def _():
        carry_ref[...] = jnp.zeros(carry_ref.shape, jnp.float32)

    xh, xl = _split_hi_lo(x_ref[...])
    wh, wl = _split_hi_lo(w_ref[...])
    logits = _dot_nt(wh, xh) + _dot_nt(wh, xl) + _dot_nt(wl, xh)
    aff = 1.0 / (1.0 + jnp.exp(-logits))
    sel = aff + bias_ref[...]
    t = sel.shape[1]

    sel3 = sel.reshape(PER_GROUP, N_GROUPS, t)
    jidx = lax.broadcasted_iota(jnp.int32, (PER_GROUP, N_GROUPS, t), 0)
    m1 = jnp.max(sel3, axis=0, keepdims=True)
    first = jnp.min(jnp.where(sel3 == m1, jidx, PER_GROUP), axis=0, keepdims=True)
    m2 = jnp.max(jnp.where(jidx == first, -jnp.inf, sel3), axis=0, keepdims=True)
    gscore = (m1 + m2)[0]

    gid = lax.broadcasted_iota(jnp.int32, (N_GROUPS, t), 0)
    grank = jnp.zeros((N_GROUPS, t), jnp.int32)
    for g2 in range(N_GROUPS):
        other = gscore[g2:g2 + 1, :]
        beats = jnp.logical_or(other > gscore, jnp.logical_and(other == gscore, g2 < gid))
        grank = grank + beats.astype(jnp.int32)
    gmask = grank < TOPK_GROUPS
    emask = jnp.broadcast_to(gmask[None], (PER_GROUP, N_GROUPS, t)).reshape(N_EXPERTS, t)
    val = jnp.where(emask, sel, -jnp.inf)

    eid = eid_ref[...]
    chosen = jnp.zeros((N_EXPERTS, t), jnp.bool_)
    for _ in range(TOP_K):
        best = jnp.max(val, axis=0, keepdims=True)
        pick = jnp.min(jnp.where(val == best, eid, N_EXPERTS), axis=0, keepdims=True)
        hit = eid == pick
        chosen = jnp.logical_or(chosen, hit)
        val = jnp.where(hit, -jnp.inf, val)
    picked = jnp.where(chosen, aff, 0.0)
    gate_ref[...] = picked / jnp.sum(picked, axis=0, keepdims=True) * ROUTED_SCALE

    ind = chosen.astype(jnp.bfloat16)
    prefix = _dot(ind, tri_ref[...])
    carry = carry_ref[...]
    rank_ref[...] = jnp.where(chosen, prefix + carry, -1.0)
    carry = carry + jnp.sum(chosen.astype(jnp.float32), axis=1, keepdims=True)
    carry_ref[...] = carry
    cnt_ref[...] = carry


def _route(x, w_router_t, bias_col, eid_col):
    n, d = x.shape
    t = min(ROUTE_T, n)
    tri = jnp.asarray(np.triu(np.ones((t, t), np.float32), 1), jnp.bfloat16)
    whole = lambda i: (0, 0)
    return pl.pallas_call(
        _route_kernel,
        grid=(n // t,),
        in_specs=[
            pl.BlockSpec((t, d), lambda i: (i, 0)),
            pl.BlockSpec((N_EXPERTS, d), whole),
            pl.BlockSpec((N_EXPERTS, 1), whole),
            pl.BlockSpec((N_EXPERTS, 1), whole),
            pl.BlockSpec((t, t), whole),
        ],
        out_specs=[
            pl.BlockSpec((N_EXPERTS, t), lambda i: (0, i)),
            pl.BlockSpec((N_EXPERTS, t), lambda i: (0, i)),
            pl.BlockSpec((N_EXPERTS, 1), whole),
        ],
        out_shape=[
            jax.ShapeDtypeStruct((N_EXPERTS, n), jnp.float32),
            jax.ShapeDtypeStruct((N_EXPERTS, n), jnp.float32),
            jax.ShapeDtypeStruct((N_EXPERTS, 1), jnp.float32),
        ],
        scratch_shapes=[pltpu.VMEM((N_EXPERTS, 1), jnp.float32)],
        compiler_params=_params(("arbitrary",)),
        name="moe_route",
    )(x, w_router_t, bias_col, eid_col, tri)


def _dest_kernel(gate_ref, rank_ref, pstart_ref, ltri_ref, dest_ref, g_ref):
    rank = rank_ref[...]
    chosen = rank >= 0.0
    slot = _dot(ltri_ref[...], chosen.astype(jnp.bfloat16))
    row = rank + pstart_ref[...]
    gate = gate_ref[...]
    dests, gates = [], []
    for k in range(TOP_K):
        hit = jnp.logical_and(chosen, slot == float(k))
        dests.append(jnp.sum(jnp.where(hit, row, 0.0), axis=0, keepdims=True))
        gates.append(jnp.sum(jnp.where(hit, gate, 0.0), axis=0, keepdims=True))
    dest_ref[...] = jnp.concatenate(dests, axis=0).astype(jnp.int32)
    g_ref[...] = jnp.concatenate(gates, axis=0)


def _dest(gate_dense, rank_dense, pstart_col):
    e, n = gate_dense.shape
    t = min(ROUTE_T, n)
    ltri = jnp.asarray(np.tril(np.ones((e, e), np.float32), -1), jnp.bfloat16)
    whole = lambda i: (0, 0)
    tile = lambda i: (0, i)
    return pl.pallas_call(
        _dest_kernel,
        grid=(n // t,),
        in_specs=[pl.BlockSpec((e, t), tile), pl.BlockSpec((e, t), tile),
                  pl.BlockSpec((e, 1), whole), pl.BlockSpec((e, e), whole)],
        out_specs=[pl.BlockSpec((TOP_K, t), tile), pl.BlockSpec((TOP_K, t), tile)],
        out_shape=[jax.ShapeDtypeStruct((TOP_K, n), jnp.int32),
                   jax.ShapeDtypeStruct((TOP_K, n), jnp.float32)],
        compiler_params=_params(("parallel",)),
        name="moe_dest",
    )(gate_dense, rank_dense, pstart_col, ltri)


def _dispatch_kernel(dest_ref, padlo_ref, padn_ref, nu_ref, xp_ref, xs_ref, zero_ref, sem, zsem, *,
                     t_tile, nblk):
    i = pl.program_id(0)

    @pl.when(i == 0)
    def _():
        zero_ref[...] = jnp.zeros(zero_ref.shape, zero_ref.dtype)

        def for_each_pad_copy(e, fn):
            off = padlo_ref[e]
            cnt = padn_ref[e]
            for bit in reversed(range(MOE_BLOCK.bit_length() - 1)):
                size = 1 << bit
                on = (cnt & size) != 0

                @pl.when(on)
                def _():
                    fn(pltpu.make_async_copy(zero_ref.at[pl.ds(0, size)], xs_ref.at[pl.ds(off, size)], zsem))
                off = off + jnp.where(on, size, 0)

        def tail_copy(b):
            return pltpu.make_async_copy(zero_ref, xs_ref.at[pl.ds(b * MOE_BLOCK, MOE_BLOCK)], zsem)

        def start_pad(e, c):
            for_each_pad_copy(e, lambda cp: cp.start())
            return c

        def wait_pad(e, c):
            for_each_pad_copy(e, lambda cp: cp.wait())
            return c

        def start_tail(b, c):
            tail_copy(b).start()
            return c

        def wait_tail(b, c):
            tail_copy(b).wait()
            return c

        lax.fori_loop(0, N_EXPERTS, start_pad, 0)
        lax.fori_loop(nu_ref[0], nblk, start_tail, 0)
        lax.fori_loop(0, N_EXPERTS, wait_pad, 0)
        lax.fori_loop(nu_ref[0], nblk, wait_tail, 0)

    for t in range(t_tile):
        for k in range(TOP_K):
            pltpu.make_async_copy(xp_ref.at[t], xs_ref.at[dest_ref[k, t]], sem).start(priority=k % 2)
    for k in range(TOP_K):
        pltpu.make_async_copy(xp_ref, xs_ref.at[pl.ds(0, t_tile)], sem).wait()


def _dispatch(xp, dest, pad_lo, pad_n, n_used, n_rows):
    n = xp.shape[0]
    t = min(DISPATCH_T, n)
    nblk = n_rows // MOE_BLOCK
    smem = pl.BlockSpec(memory_space=pltpu.SMEM)
    return pl.pallas_call(
        functools.partial(_dispatch_kernel, t_tile=t, nblk=nblk),
        grid=(n // t,),
        in_specs=[
            pl.BlockSpec((TOP_K, t), lambda i: (0, i), memory_space=pltpu.SMEM),
            smem, smem, smem,
            pl.BlockSpec((t, ROW_SLAB, 128), lambda i: (i, 0, 0)),
        ],
        out_specs=pl.BlockSpec(memory_space=pl.ANY),
        out_shape=jax.ShapeDtypeStruct((n_rows, ROW_SLAB, 128), jnp.uint32),
        scratch_shapes=[pltpu.VMEM((MOE_BLOCK, ROW_SLAB, 128), jnp.uint32),
                        pltpu.SemaphoreType.DMA(()), pltpu.SemaphoreType.DMA(())],
        compiler_params=_params(("arbitrary",)),
        name="moe_dispatch",
    )(dest, pad_lo, pad_n, n_used, xp)


def _experts_kernel(be_ref, nu_ref, xs_ref, wg_ref, wu_ref, wd_ref, ys_ref, wgc_ref, wuc_ref, wdc_ref, h_ref,
                    *, nblk):
    b = pl.program_id(0)
    nu = nu_ref[0]

    def starts_run(j):
        return jnp.logical_or(j == 0, be_ref[j] != be_ref[jnp.maximum(j - 1, 0)])

    @pl.when(jnp.logical_and(b < nu, starts_run(jnp.minimum(b, nblk - 1))))
    def _():
        wgc_ref[...] = wg_ref[0, 0].astype(MXU_DTYPE)
        wuc_ref[...] = wu_ref[0, 0].astype(MXU_DTYPE)

    @pl.when(jnp.logical_and(jnp.logical_and(b >= 1, b <= nu), starts_run(jnp.maximum(b - 1, 0))))
    def _():
        wdc_ref[...] = wd_ref[0, 0].astype(MXU_DTYPE)

    def up_half():
        xb = jnp.concatenate(_unpack_rows(xs_ref, 0, MOE_BLOCK), axis=1).astype(MXU_DTYPE)
        g = _dot(xb, wgc_ref[...])
        u = _dot(xb, wuc_ref[...])
        return (_silu(g) * u).astype(MXU_DTYPE)

    def down_half(hmid):
        _pack_rows(_dot(hmid, wdc_ref[...]), ys_ref, 0, MOE_BLOCK)

    @pl.when(b == 0)
    def _():
        h_ref[...] = up_half()

    @pl.when(jnp.logical_and(b >= 1, b < nu))
    def _():
        h_prev = h_ref[...]
        h_new = up_half()
        down_half(h_prev)
        h_ref[...] = h_new

    @pl.when(b == nu)
    def _():
        down_half(h_ref[...])

    @pl.when(b > nu)
    def _():
        ys_ref[...] = jnp.zeros(ys_ref.shape, ys_ref.dtype)


def _experts(xs2d, block_e, n_used, w_gate, w_up, w_down, layer):
    d, hid = w_gate.shape[2], w_gate.shape[3]
    blk_rows = MOE_BLOCK * ROW_SLAB
    nblk = xs2d.shape[0] // blk_rows
    out_blk = lambda b, be, nu: (jnp.maximum(b - 1, 0), 0)
    used_blk = lambda b, be, nu: (jnp.minimum(b, nu[0] - 1), 0)
    expert = lambda b, be, nu: (layer, be[jnp.minimum(b, nblk - 1)], 0, 0)
    prev_expert = lambda b, be, nu: (layer, be[jnp.maximum(b - 1, 0)], 0, 0)
    grid_spec = pltpu.PrefetchScalarGridSpec(
        num_scalar_prefetch=2,
        grid=(nblk + 1,),
        in_specs=[
            pl.BlockSpec((blk_rows, 128), used_blk),
            pl.BlockSpec((1, 1, d, hid), expert),
            pl.BlockSpec((1, 1, d, hid), expert),
            pl.BlockSpec((1, 1, hid, d), prev_expert),
        ],
        out_specs=pl.BlockSpec((blk_rows, 128), out_blk),
        scratch_shapes=[pltpu.VMEM((d, hid), MXU_DTYPE), pltpu.VMEM((d, hid), MXU_DTYPE),
                        pltpu.VMEM((hid, d), MXU_DTYPE), pltpu.VMEM((MOE_BLOCK, hid), MXU_DTYPE)],
    )
    return pl.pallas_call(
        functools.partial(_experts_kernel, nblk=nblk),
        grid_spec=grid_spec,
        out_shape=jax.ShapeDtypeStruct(xs2d.shape, jnp.uint32),
        compiler_params=_params(("arbitrary",)),
        name="moe_experts",
    )(block_e, n_used, xs2d, w_gate, w_up, w_down)


def _combine_kernel(dest_ref, dnext_ref, g_ref, x_ref, wsg_ref, wsu_ref, wsd_ref, lg_ref, lb_ref, ys_ref,
                    ys2d_ref, o_ref, buf_ref, sem, *, t_tile):
    i = pl.program_id(0)
    last = pl.num_programs(0) - 1
    slot_rows = TOP_K * t_tile
    slot = lax.rem(i, 2)

    def gather(dref, s, unrolled):
        base = s * slot_rows

        def issue(t, carry):
            for k in range(TOP_K):
                r = pl.multiple_of((base + k * t_tile + t) * ROW_SLAB, ROW_SLAB)
                pltpu.make_async_copy(ys_ref.at[dref[k, t]], buf_ref.at[pl.ds(r, ROW_SLAB), :],
                                      sem).start(priority=k % 2)
            return carry

        if unrolled:
            for t in range(t_tile):
                issue(t, 0)
        else:
            lax.fori_loop(0, t_tile, issue, 0)

    def wait(s):
        n = t_tile * ROW_SLAB
        for k in range(TOP_K):
            r = pl.multiple_of((s * slot_rows + k * t_tile) * ROW_SLAB, ROW_SLAB)
            pltpu.make_async_copy(ys2d_ref.at[pl.ds(0, n), :], buf_ref.at[pl.ds(r, n), :], sem).wait()

    @pl.when(i == 0)
    def _():
        gather(dest_ref, 0, False)

    wait(slot)
    gather(dnext_ref, 1 - slot, True)

    x = x_ref[...]
    xb = x.astype(MXU_DTYPE)
    hmid = (_silu(_dot(xb, wsg_ref[...])) * _dot(xb, wsu_ref[...])).astype(MXU_DTYPE)
    z = DN_ALPHA * x + _dot(hmid, wsd_ref[...])

    gt = jnp.transpose(g_ref[...])
    cols = None
    for k in range(TOP_K):
        tiles = _unpack_rows(buf_ref, slot * slot_rows + k * t_tile, t_tile)
        gk = gt[:, k:k + 1]
        cols = [tl * gk for tl in tiles] if cols is None else [c + tl * gk for c, tl in zip(cols, tiles)]
    z = z + jnp.concatenate(cols, axis=1)
    o_ref[...] = _layer_norm(z, lg_ref[...], lb_ref[...])

    @pl.when(i == last)
    def _():
        wait(1 - slot)


def _combine(x, dest, gates, ys, ws_gate, ws_up, ws_down, ln_g, ln_b):
    n, d = x.shape
    t = min(COMBINE_T, n)
    steps = n // t
    whole = lambda i: (0, 0)
    tile = lambda i: (0, i)
    row = lambda i: (i, 0)
    nxt = lambda i: (0, jnp.minimum(i + 1, steps - 1))
    return pl.pallas_call(
        functools.partial(_combine_kernel, t_tile=t),
        grid=(steps,),
        in_specs=[
            pl.BlockSpec((TOP_K, t), tile, memory_space=pltpu.SMEM),
            pl.BlockSpec((TOP_K, t), nxt, memory_space=pltpu.SMEM),
            pl.BlockSpec((TOP_K, t), tile),
            pl.BlockSpec((t, d), row),
            pl.BlockSpec(ws_gate.shape, whole),
            pl.BlockSpec(ws_up.shape, whole),
            pl.BlockSpec(ws_down.shape, whole),
            pl.BlockSpec((1, d), whole),
            pl.BlockSpec((1, d), whole),
            pl.BlockSpec(memory_space=pl.ANY),
            pl.BlockSpec(memory_space=pl.ANY),
        ],
        out_specs=pl.BlockSpec((t, d), row),
        out_shape=jax.ShapeDtypeStruct((n, d), jnp.float32),
        scratch_shapes=[pltpu.VMEM((2 * TOP_K * t * ROW_SLAB, 128), jnp.uint32), pltpu.SemaphoreType.DMA(())],
        compiler_params=_params(("arbitrary",)),
        name="moe_combine",
    )(dest, dest, gates, x, ws_gate, ws_up, ws_down, ln_g, ln_b, ys, ys.reshape(-1, 128))


def _moe(x, xp, w_router_t, bias_col, eid_col, inv_perm, w_gate, w_up, w_down, layer, ws_gate, ws_up, ws_down,
         ln_g, ln_b):
    n, d = x.shape
    gate_dense, rank_dense, counts = _route(x, w_router_t, bias_col, eid_col)
    cnt = counts[:, 0].astype(jnp.int32)[inv_perm]
    padded = (cnt + MOE_BLOCK - 1) // MOE_BLOCK * MOE_BLOCK
    pends = jnp.cumsum(padded)
    pstarts = pends - padded
    nk = n * TOP_K
    n_rows = (nk + N_EXPERTS * (MOE_BLOCK - 1) + MOE_BLOCK - 1) // MOE_BLOCK * MOE_BLOCK
    nblk = n_rows // MOE_BLOCK
    blk_start = jnp.arange(nblk, dtype=jnp.int32) * MOE_BLOCK
    block_e = jnp.minimum(jnp.sum(blk_start[:, None] >= pends[None, :], axis=1), N_EXPERTS - 1).astype(jnp.int32)
    n_used = (pends[-1:] // MOE_BLOCK).astype(jnp.int32)
    perm = jnp.argsort(inv_perm)
    pstart_col = pstarts[perm].astype(jnp.float32)[:, None]
    dest, gates = _dest(gate_dense, rank_dense, pstart_col)
    xs = _dispatch(xp, dest, (pstarts + cnt).astype(jnp.int32), (padded - cnt).astype(jnp.int32), n_used, n_rows)
    ys = _experts(xs.reshape(n_rows * ROW_SLAB, 128), block_e, n_used, w_gate, w_up, w_down, layer)
    return _combine(x, dest, gates, ys.reshape(n_rows, ROW_SLAB, 128), ws_gate, ws_up, ws_down, ln_g, ln_b)


def kernel(x, mem, w_in_a, sinks, w_in_b, lambda_qk, subln_g, w_mem_kv, w_out, ln_g, ln_b, w_router,
           router_bias, w_gate, w_up, w_down, w_shared_gate, w_shared_up, w_shared_down):
    batch, seq, d = x.shape
    n = batch * seq
    cdt = MXU_DTYPE
    qa, kva = SWA_Q_HEADS * HEAD_DIM, SWA_KV_HEADS * HEAD_DIM
    qb, vb = DIFF_HEADS * 2 * HEAD_DIM, DIFF_HEADS * DIFF_V_DIM

    rows = np.arange(N_EXPERTS)
    perm = (rows % N_GROUPS) * PER_GROUP + rows // N_GROUPS
    inv_perm = jnp.asarray(np.argsort(perm), jnp.int32)
    eid_col = jnp.asarray(perm[:, None], jnp.int32)

    xf = x.reshape(n, d)
    kv_all = _mem_kv(mem.reshape(-1, d), w_mem_kv)
    mlen = mem.shape[1]
    kv_all = kv_all.reshape(DEPTH * batch, mlen, kv_all.shape[-1])

    def in_proj_of(i):
        j = i // 2
        if i % 2 == 0:
            w = w_in_a[j].astype(cdt)
            return ([w[:, :qa], w[:, qa:qa + kva], w[:, qa + kva:qa + 2 * kva], w[:, qa + 2 * kva:]],
                    (QK_SCALE, 1.0, 1.0, QK_SCALE))
        w = w_in_b[j].astype(cdt)
        return ([w[:, :qb], w[:, qb:2 * qb], w[:, 2 * qb:2 * qb + vb], w[:, 2 * qb + vb:]],
                (QK_SCALE * LOG2E, 1.0, 1.0, QK_SCALE))

    for i in range(DEPTH):
        j = i // 2
        q, k, v, qm = _project(xf, *in_proj_of(i))
        if i % 2 == 0:
            mix = _swa_attention(q, k, v, sinks[j], batch, seq)
        else:
            lam_init = 0.8 - 0.6 * math.exp(-0.3 * i)
            mix = _diff_attention(q, k, v, lambda_qk[j], subln_g[j][None, :], batch, seq, lam_init)
        kv_l = kv_all[i * batch:(i + 1) * batch]
        xf, xp = _out_proj(mix, qm, kv_l, w_out[i].astype(cdt), xf, ln_g[i, 0][None, :], ln_b[i, 0][None, :],
                           batch, seq)
        w_router_t = jnp.transpose(w_router[i])[perm]
        bias_col = router_bias[i][perm][:, None]
        xf = _moe(xf, xp, w_router_t, bias_col, eid_col, inv_perm,
                  w_gate, w_up, w_down, i,
                  w_shared_gate[i].astype(cdt), w_shared_up[i].astype(cdt), w_shared_down[i].astype(cdt),
                  ln_g[i, 1][None, :], ln_b[i, 1][None, :])
    return xf.reshape(batch, seq, d)
```

```python
import functools
import math

import jax
import jax.numpy as jnp
import numpy as np
from jax import lax
from jax.experimental import pallas as pl
from jax.experimental.pallas import tpu as pltpu

HEAD_DIM = 64
MEM_HEADS = 4
SWA_Q_HEADS = 12
SWA_KV_HEADS = 3
SWA_GROUP = SWA_Q_HEADS // SWA_KV_HEADS
WINDOW = 128
DIFF_HEADS = 6
DIFF_V_DIM = 2 * HEAD_DIM
N_EXPERTS = 64
TOP_K = 8
N_GROUPS = 8
PER_GROUP = N_EXPERTS // N_GROUPS
TOPK_GROUPS = 4
ROUTED_SCALE = 2.5
MOE_BLOCK = 1024
DEPTH = 4
DN_ALPHA = (2.0 * DEPTH) ** 0.25
LN_EPS = 1e-5
NEG = -1e30
QK_SCALE = HEAD_DIM ** -0.5

MXU_DTYPE = jnp.bfloat16

PROJ_TM = 512
DIFF_TQ = 1024
ROUTE_T = 512
DISPATCH_T = 256
COMBINE_T = 256
VMEM_LIMIT = 56 * 1024 * 1024


def _alibi_slopes(n):
    def pow2_slopes(m):
        start = 2.0 ** (-8.0 / m)
        return [start ** (i + 1) for i in range(m)]
    if (n & (n - 1)) == 0:
        return pow2_slopes(n)
    c = 2 ** int(math.floor(math.log2(n)))
    return pow2_slopes(c) + pow2_slopes(2 * c)[0::2][: n - c]


def _dot(a, b):
    return jnp.dot(a, b, preferred_element_type=jnp.float32)


def _dot_nt(a, b):
    return lax.dot_general(a, b, (((1,), (1,)), ((), ())), preferred_element_type=jnp.float32)


def _params(sem):
    return pltpu.CompilerParams(dimension_semantics=sem, vmem_limit_bytes=VMEM_LIMIT)


def _layer_norm(z, g, b):
    mu = jnp.mean(z, axis=-1, keepdims=True)
    zc = z - mu
    var = jnp.mean(zc * zc, axis=-1, keepdims=True)
    return zc * lax.rsqrt(var + LN_EPS) * g + b


def _silu(x):
    return x / (1.0 + jnp.exp(-x))


def _proj_kernel(*refs, scales):
    nw = len(scales)
    x_ref = refs[0]
    w_refs = refs[1:1 + nw]
    o_refs = refs[1 + nw:]
    xb = x_ref[...].astype(MXU_DTYPE)
    for w_ref, o_ref, sc in zip(w_refs, o_refs, scales):
        acc = _dot(xb, w_ref[...])
        if sc != 1.0:
            acc = acc * sc
        o_ref[...] = acc.astype(o_ref.dtype)


def _project(x, weights, scales):
    n, d = x.shape
    tm = min(PROJ_TM, n)
    in_specs = [pl.BlockSpec((tm, d), lambda i: (i, 0))]
    in_specs += [pl.BlockSpec(w.shape, lambda i: (0, 0)) for w in weights]
    out_specs = [pl.BlockSpec((tm, w.shape[1]), lambda i: (i, 0)) for w in weights]
    out_shape = [jax.ShapeDtypeStruct((n, w.shape[1]), MXU_DTYPE) for w in weights]
    return pl.pallas_call(
        functools.partial(_proj_kernel, scales=tuple(scales)),
        grid=(n // tm,),
        in_specs=in_specs,
        out_specs=out_specs,
        out_shape=out_shape,
        compiler_params=_params(("parallel",)),
        name="in_proj",
    )(x, *weights)


def _swa_kernel(sinks_ref, bias_ref, q_ref, kp_ref, kc_ref, vp_ref, vc_ref, o_ref):
    i = pl.program_id(1)
    q = q_ref[...]
    kp, kc, vp, vc = kp_ref[...], kc_ref[...], vp_ref[...], vc_ref[...]
    rows = SWA_GROUP * WINDOW
    head_of_row = lax.broadcasted_iota(jnp.int32, (rows, 1), 0) // WINDOW
    vpad = jnp.concatenate([jnp.zeros((WINDOW, HEAD_DIM), MXU_DTYPE),
                            jnp.ones((WINDOW, 2 * HEAD_DIM), MXU_DTYPE)], axis=1)
    outs = []
    for kv in range(SWA_KV_HEADS):
        ks = slice(kv * HEAD_DIM, (kv + 1) * HEAD_DIM)
        kph, kch, vph, vch = kp[:, ks], kc[:, ks], vp[:, ks], vc[:, ks]
        h0 = kv * SWA_GROUP
        qg = jnp.concatenate([q[:, (h0 + g) * HEAD_DIM:(h0 + g + 1) * HEAD_DIM] for g in range(SWA_GROUP)],
                             axis=0)
        sink = jnp.zeros((rows, 1), jnp.float32) + sinks_ref[h0]
        for g in range(1, SWA_GROUP):
            sink = jnp.where(head_of_row == g, sinks_ref[h0 + g], sink)
        sp = jnp.where(i > 0, _dot_nt(qg, kph) + bias_ref[kv, 0], NEG)
        sc = _dot_nt(qg, kch) + bias_ref[kv, 1]
        m = jnp.maximum(jnp.maximum(jnp.max(sp, axis=-1, keepdims=True),
                                    jnp.max(sc, axis=-1, keepdims=True)), sink)
        pp = jnp.exp(sp - m).astype(MXU_DTYPE)
        pc = jnp.exp(sc - m).astype(MXU_DTYPE)
        ov = _dot(pp, jnp.concatenate([vph, vpad], axis=1)) + _dot(pc, jnp.concatenate([vch, vpad], axis=1))
        denom = ov[:, 2 * HEAD_DIM:3 * HEAD_DIM] + jnp.exp(sink - m)
        o = ov[:, :HEAD_DIM] / denom
        outs += [o[g * WINDOW:(g + 1) * WINDOW] for g in range(SWA_GROUP)]
    o_ref[...] = jnp.concatenate(outs, axis=-1).astype(o_ref.dtype)


def _swa_bias():
    slopes = np.asarray(_alibi_slopes(SWA_Q_HEADS), np.float64).reshape(SWA_KV_HEADS, SWA_GROUP)
    qi = np.arange(WINDOW)[:, None]
    kj = np.arange(WINDOW)[None, :]
    dist_c = qi - kj
    dist_p = dist_c + WINDOW
    out = np.empty((SWA_KV_HEADS, 2, SWA_GROUP, WINDOW, WINDOW), np.float32)
    for kv in range(SWA_KV_HEADS):
        for g in range(SWA_GROUP):
            out[kv, 0, g] = np.where(dist_p < WINDOW, -slopes[kv, g] * dist_p, NEG)
            out[kv, 1, g] = np.where(dist_c >= 0, -slopes[kv, g] * dist_c, NEG)
    return jnp.asarray(out.reshape(SWA_KV_HEADS, 2, SWA_GROUP * WINDOW, WINDOW))


def _swa_attention(q, k, v, sinks, batch, seq):
    n = batch * seq
    nb = seq // WINDOW
    qw, kw = q.shape[1], k.shape[1]
    cur = lambda b, i: (b * nb + i, 0)
    prev = lambda b, i: (b * nb + jnp.maximum(i - 1, 0), 0)
    bias = _swa_bias()
    return pl.pallas_call(
        _swa_kernel,
        grid=(batch, nb),
        in_specs=[
            pl.BlockSpec(memory_space=pltpu.SMEM),
            pl.BlockSpec(bias.shape, lambda b, i: (0, 0, 0, 0)),
            pl.BlockSpec((WINDOW, qw), cur),
            pl.BlockSpec((WINDOW, kw), prev),
            pl.BlockSpec((WINDOW, kw), cur),
            pl.BlockSpec((WINDOW, kw), prev),
            pl.BlockSpec((WINDOW, kw), cur),
        ],
        out_specs=pl.BlockSpec((WINDOW, qw), cur),
        out_shape=jax.ShapeDtypeStruct((n, qw), MXU_DTYPE),
        compiler_params=_params(("parallel", "parallel")),
        name="swa_attention",
    )(sinks, bias, q, k, k, v, v)


LOG2E = math.log2(math.e)
_BIAS_PIECES = 3


def _bf16_pieces(c):
    out, rest = [], np.float64(c)
    for _ in range(_BIAS_PIECES):
        piece = np.asarray(rest, np.float32).astype(jnp.bfloat16).astype(np.float64)
        out.append(float(piece))
        rest = rest - piece
    return out


def _alibi_templates(tq):
    idx = np.arange(tq)
    hi, lo = (idx // 256) * 256, idx % 256
    hw = 2 * HEAD_DIM
    qt = np.zeros((DIFF_HEADS, 2, tq, hw), np.float32)
    kt = np.zeros((DIFF_HEADS, 2, tq, hw), np.float32)
    for h, slope in enumerate(_alibi_slopes(DIFF_HEADS)):
        for c in range(2):
            base = HEAD_DIM if c == 0 else 0
            for p, piece in enumerate(_bf16_pieces(slope * LOG2E)):
                qt[h, c, :, base + p] = -hi
                kt[h, c, :, base + p] = piece
                qt[h, c, :, base + _BIAS_PIECES + p] = -lo
                kt[h, c, :, base + _BIAS_PIECES + p] = piece
                qt[h, c, :, base + 2 * _BIAS_PIECES + p] = piece
                kt[h, c, :, base + 2 * _BIAS_PIECES + p] = hi
                qt[h, c, :, base + 3 * _BIAS_PIECES + p] = piece
                kt[h, c, :, base + 3 * _BIAS_PIECES + p] = lo
    return jnp.asarray(qt, jnp.bfloat16), jnp.asarray(kt, jnp.bfloat16)


def _diff_kernel(slopes_ref, q_ref, k_ref, v_ref, qt_ref, kt_ref, ones_ref, dmask_ref, lam_ref, g_ref,
                 o_ref, qa_ref, m_ref, acc_ref, *, tq, lam_init):
    h = pl.program_id(1)
    qi = pl.program_id(2)
    slope2 = slopes_ref[h]
    hw = 2 * HEAD_DIM
    lane = lax.broadcasted_iota(jnp.int32, (1, hw), 1)
    first = lane < HEAD_DIM
    q = q_ref[...]
    qa_ref[0] = jnp.where(first, q, qt_ref[0, 0])
    qa_ref[1] = jnp.where(first, qt_ref[0, 1], q)
    m_ref[...] = jnp.full(m_ref.shape, NEG, jnp.float32)
    acc_ref[...] = jnp.zeros(acc_ref.shape, jnp.float32)
    ones = ones_ref[...]

    def softmax_pv(c, u, vaug, shift):
        m_old = m_ref[c]
        m_new = jnp.maximum(m_old, jnp.max(u, axis=-1, keepdims=True) + shift)
        p = jnp.exp2(u - (m_new - shift)).astype(MXU_DTYPE)
        alpha = jnp.exp2(m_old - m_new)
        acc_ref[c] = alpha * acc_ref[c] + _dot(p, vaug)
        m_ref[c] = m_new

    def process(ki, tiles, diagonal):
        vaugs, us, shifts = [], [], []
        for s in range(tiles):
            start = pl.multiple_of((ki + s) * tq, tq)
            k = k_ref[pl.ds(start, tq), :]
            v = v_ref[pl.ds(start, tq), :]
            vaugs.append(jnp.concatenate([v, ones], axis=1))
            kas = (jnp.where(first, k, kt_ref[0, 0]), jnp.where(first, kt_ref[0, 1], k))
            us.append([_dot_nt(qa_ref[c], kas[c]) for c in range(2)])
            shifts.append(slope2 * ((ki + s - qi) * tq).astype(jnp.float32))
        if diagonal:
            us = [[u + dmask_ref[...] for u in pair] for pair in us]
        for s in range(tiles):
            for c in range(2):
                softmax_pv(c, us[s][c], vaugs[s], shifts[s])

    def two_tiles(it, carry):
        process(2 * it, 2, False)
        return carry

    def one_tile(ki, carry):
        process(ki, 1, False)
        return carry

    pairs = qi // 2
    lax.fori_loop(0, pairs, two_tiles, 0)
    lax.fori_loop(2 * pairs, qi, one_tile, 0)
    process(qi, 1, True)

    lp = lam_ref[...]
    lam = (jnp.exp(jnp.sum(lp[0:1] * lp[1:2], axis=-1, keepdims=True))
           - jnp.exp(jnp.sum(lp[2:3] * lp[3:4], axis=-1, keepdims=True)) + lam_init)
    a0, a1 = acc_ref[0], acc_ref[1]
    o = a0[:, :hw] / a0[:, hw:] - lam * (a1[:, :hw] / a1[:, hw:])
    ms = jnp.mean(o * o, axis=-1, keepdims=True)
    o = o * lax.rsqrt(ms + LN_EPS) * g_ref[...] * (1.0 - lam_init)
    o_ref[...] = o.astype(o_ref.dtype)


def _diff_attention(q, k, v, lam_qk, subln_g, batch, seq, lam_init):
    n = batch * seq
    tq = min(DIFF_TQ, seq)
    nq = seq // tq
    hw = 2 * HEAD_DIM
    slopes2 = np.asarray(_alibi_slopes(DIFF_HEADS), np.float64) * LOG2E
    qt, kt = _alibi_templates(tq)
    ones_col = np.ones((tq, hw), np.float32)
    rel = np.arange(tq)[None, :] - np.arange(tq)[:, None]
    dmask = jnp.asarray(np.where(rel <= 0, 0.0, NEG).astype(np.float32))
    kernel = functools.partial(_diff_kernel, tq=tq, lam_init=lam_init)
    fixed = lambda b, h, i: (0, 0)
    per_head = lambda b, h, i: (h, 0, 0, 0)
    return pl.pallas_call(
        kernel,
        grid=(batch, DIFF_HEADS, nq),
        in_specs=[
            pl.BlockSpec(memory_space=pltpu.SMEM),
            pl.BlockSpec((tq, hw), lambda b, h, i: (b * nq + i, h)),
            pl.BlockSpec((seq, hw), lambda b, h, i: (b, h)),
            pl.BlockSpec((seq, hw), lambda b, h, i: (b, h)),
            pl.BlockSpec((1, 2, tq, hw), per_head),
            pl.BlockSpec((1, 2, tq, hw), per_head),
            pl.BlockSpec((tq, hw), fixed),
            pl.BlockSpec((tq, tq), fixed),
            pl.BlockSpec(lam_qk.shape, fixed),
            pl.BlockSpec(subln_g.shape, fixed),
        ],
        out_specs=pl.BlockSpec((tq, hw), lambda b, h, i: (b * nq + i, h)),
        out_shape=jax.ShapeDtypeStruct((n, DIFF_HEADS * hw), MXU_DTYPE),
        scratch_shapes=[
            pltpu.VMEM((2, tq, hw), MXU_DTYPE),
            pltpu.VMEM((2, tq, 1), jnp.float32),
            pltpu.VMEM((2, tq, 2 * hw), jnp.float32),
        ],
        compiler_params=_params(("parallel", "parallel", "arbitrary")),
        name="diff_attention",
    )(jnp.asarray(slopes2, jnp.float32), q, k, v, qt, kt, jnp.asarray(ones_col, MXU_DTYPE), dmask,
      lam_qk, subln_g)


def _memkv_kernel(mem_ref, w_ref, o_ref):
    o_ref[0] = _dot(mem_ref[...].astype(MXU_DTYPE), w_ref[0].astype(MXU_DTYPE)).astype(o_ref.dtype)


def _mem_kv(mem2d, w_mem_kv):
    depth, d, w = w_mem_kv.shape
    rows = mem2d.shape[0]
    return pl.pallas_call(
        _memkv_kernel,
        grid=(depth,),
        in_specs=[pl.BlockSpec((rows, d), lambda l: (0, 0)),
                  pl.BlockSpec((1, d, w), lambda l: (l, 0, 0))],
        out_specs=pl.BlockSpec((1, rows, w), lambda l: (l, 0, 0)),
        out_shape=jax.ShapeDtypeStruct((depth, rows, w), MXU_DTYPE),
        compiler_params=_params(("parallel",)),
        name="mem_kv_proj",
    )(mem2d, w_mem_kv)


ROW_SLAB = 4
_HI16 = 0xFFFF0000


def _bf16_bits_hi(x):
    b = lax.bitcast_convert_type(x, jnp.uint32)
    b = b + jnp.uint32(0x7FFF) + ((b >> 16) & jnp.uint32(1))
    return b & jnp.uint32(_HI16)


def _pack_rows(z, ref, base, t):
    half = z.shape[1] // 2
    words = _bf16_bits_hi(z[:, half:]) | (_bf16_bits_hi(z[:, :half]) >> 16)
    for s in range(ROW_SLAB):
        ref[pl.ds(base * ROW_SLAB + s, t, stride=ROW_SLAB), :] = words[:, s * 128:(s + 1) * 128]


def _unpack_rows(ref, base, t):
    lo, hi = [], []
    for s in range(ROW_SLAB):
        w = ref[pl.ds(base * ROW_SLAB + s, t, stride=ROW_SLAB), :]
        lo.append(lax.bitcast_convert_type(w << 16, jnp.float32))
        hi.append(lax.bitcast_convert_type(w & jnp.uint32(_HI16), jnp.float32))
    return lo + hi


def _outproj_kernel(mix_ref, qm_ref, kv_ref, wo_ref, x_ref, g_ref, b_ref, o_ref, op_ref):
    qm = qm_ref[...]
    kvm = kv_ref[0]
    mw = MEM_HEADS * HEAD_DIM
    heads = [slice(h * HEAD_DIM, (h + 1) * HEAD_DIM) for h in range(MEM_HEADS)]
    scores = [_dot_nt(qm[:, hs], kvm[:, hs]) for hs in heads]
    tops = [jnp.max(s, axis=-1, keepdims=True) for s in scores]
    probs = [jnp.exp(s - m) for s, m in zip(scores, tops)]
    sums = [jnp.sum(p, axis=-1, keepdims=True) for p in probs]
    outs = [_dot(p.astype(MXU_DTYPE), kvm[:, mw + hs.start: mw + hs.stop]) / l
            for p, hs, l in zip(probs, heads, sums)]
    mem_out = jnp.concatenate(outs, axis=-1).astype(MXU_DTYPE)
    mixw = mix_ref.shape[1]
    y = _dot(mix_ref[...], wo_ref[0:mixw, :]) + _dot(mem_out, wo_ref[mixw:, :])
    z = DN_ALPHA * x_ref[...] + y
    zn = _layer_norm(z, g_ref[...], b_ref[...])
    o_ref[...] = zn
    _pack_rows(zn, op_ref, 0, zn.shape[0])


def _out_proj(mix, qm, kv_l, w_out, x, ln_g, ln_b, batch, seq):
    n, d = x.shape
    tm = min(PROJ_TM, seq)
    per_b = seq // tm
    mlen, kvw = kv_l.shape[1], kv_l.shape[2]
    row = lambda i: (i, 0)
    whole = lambda i: (0, 0)
    xn, xp = pl.pallas_call(
        _outproj_kernel,
        grid=(n // tm,),
        in_specs=[
            pl.BlockSpec((tm, mix.shape[1]), row),
            pl.BlockSpec((tm, qm.shape[1]), row),
            pl.BlockSpec((1, mlen, kvw), lambda i: (i // per_b, 0, 0)),
            pl.BlockSpec(w_out.shape, whole),
            pl.BlockSpec((tm, d), row),
            pl.BlockSpec((1, d), whole),
            pl.BlockSpec((1, d), whole),
        ],
        out_specs=[pl.BlockSpec((tm, d), row), pl.BlockSpec((tm * ROW_SLAB, 128), row)],
        out_shape=[jax.ShapeDtypeStruct((n, d), jnp.float32),
                   jax.ShapeDtypeStruct((n * ROW_SLAB, 128), jnp.uint32)],
        compiler_params=_params(("parallel",)),
        name="out_proj_ln",
    )(mix, qm, kv_l, w_out, x, ln_g, ln_b)
    return xn, xp.reshape(n, ROW_SLAB, 128)


def _split_hi_lo(a):
    hi = a.astype(jnp.bfloat16)
    lo = (a - hi.astype(jnp.float32)).astype(jnp.bfloat16)
    return hi, lo


def _route_kernel(x_ref, w_ref, bias_ref, eid_ref, tri_ref, gate_ref, rank_ref, cnt_ref, carry_ref):
    i = pl.program_id(0)

    @pl.when(i == 0)
    def _():
        carry_ref[...] = jnp.zeros(carry_ref.shape, jnp.float32)

    xh, xl = _split_hi_lo(x_ref[...])
    wh, wl = _split_hi_lo(w_ref[...])
    logits = _dot_nt(wh, xh) + _dot_nt(wh, xl) + _dot_nt(wl, xh)
    aff = 1.0 / (1.0 + jnp.exp(-logits))
    sel = aff + bias_ref[...]
    t = sel.shape[1]

    sel3 = sel.reshape(PER_GROUP, N_GROUPS, t)
    jidx = lax.broadcasted_iota(jnp.int32, (PER_GROUP, N_GROUPS, t), 0)
    m1 = jnp.max(sel3, axis=0, keepdims=True)
    first = jnp.min(jnp.where(sel3 == m1, jidx, PER_GROUP), axis=0, keepdims=True)
    m2 = jnp.max(jnp.where(jidx == first, -jnp.inf, sel3), axis=0, keepdims=True)
    gscore = (m1 + m2)[0]

    gid = lax.broadcasted_iota(jnp.int32, (N_GROUPS, t), 0)
    grank = jnp.zeros((N_GROUPS, t), jnp.int32)
    for g2 in range(N_GROUPS):
        other = gscore[g2:g2 + 1, :]
        beats = jnp.logical_or(other > gscore, jnp.logical_and(other == gscore, g2 < gid))
        grank = grank + beats.astype(jnp.int32)
    gmask = grank < TOPK_GROUPS
    emask = jnp.broadcast_to(gmask[None], (PER_GROUP, N_GROUPS, t)).reshape(N_EXPERTS, t)
    val = jnp.where(emask, sel, -jnp.inf)

    eid = eid_ref[...]
    chosen = jnp.zeros((N_EXPERTS, t), jnp.bool_)
    for _ in range(TOP_K):
        best = jnp.max(val, axis=0, keepdims=True)
        pick = jnp.min(jnp.where(val == best, eid, N_EXPERTS), axis=0, keepdims=True)
        hit = eid == pick
        chosen = jnp.logical_or(chosen, hit)
        val = jnp.where(hit, -jnp.inf, val)
    picked = jnp.where(chosen, aff, 0.0)
    gate_ref[...] = picked / jnp.sum(picked, axis=0, keepdims=True) * ROUTED_SCALE

    ind = chosen.astype(jnp.bfloat16)
    prefix = _dot(ind, tri_ref[...])
    carry = carry_ref[...]
    rank_ref[...] = jnp.where(chosen, prefix + carry, -1.0)
    carry = carry + jnp.sum(chosen.astype(jnp.float32), axis=1, keepdims=True)
    carry_ref[...] = carry
    cnt_ref[...] = carry


def _route(x, w_router_t, bias_col, eid_col):
    n, d = x.shape
    t = min(ROUTE_T, n)
    tri = jnp.asarray(np.triu(np.ones((t, t), np.float32), 1), jnp.bfloat16)
    whole = lambda i: (0, 0)
    return pl.pallas_call(
        _route_kernel,
        grid=(n // t,),
        in_specs=[
            pl.BlockSpec((t, d), lambda i: (i, 0)),
            pl.BlockSpec((N_EXPERTS, d), whole),
            pl.BlockSpec((N_EXPERTS, 1), whole),
            pl.BlockSpec((N_EXPERTS, 1), whole),
            pl.BlockSpec((t, t), whole),
        ],
        out_specs=[
            pl.BlockSpec((N_EXPERTS, t), lambda i: (0, i)),
            pl.BlockSpec((N_EXPERTS, t), lambda i: (0, i)),
            pl.BlockSpec((N_EXPERTS, 1), whole),
        ],
        out_shape=[
            jax.ShapeDtypeStruct((N_EXPERTS, n), jnp.float32),
            jax.ShapeDtypeStruct((N_EXPERTS, n), jnp.float32),
            jax.ShapeDtypeStruct((N_EXPERTS, 1), jnp.float32),
        ],
        scratch_shapes=[pltpu.VMEM((N_EXPERTS, 1), jnp.float32)],
        compiler_params=_params(("arbitrary",)),
        name="moe_route",
    )(x, w_router_t, bias_col, eid_col, tri)


def _dest_kernel(gate_ref, rank_ref, pstart_ref, ltri_ref, dest_ref, g_ref):
    rank = rank_ref[...]
    chosen = rank >= 0.0
    slot = _dot(ltri_ref[...], chosen.astype(jnp.bfloat16))
    row = rank + pstart_ref[...]
    gate = gate_ref[...]
    dests, gates = [], []
    for k in range(TOP_K):
        hit = jnp.logical_and(chosen, slot == float(k))
        dests.append(jnp.sum(jnp.where(hit, row, 0.0), axis=0, keepdims=True))
        gates.append(jnp.sum(jnp.where(hit, gate, 0.0), axis=0, keepdims=True))
    dest_ref[...] = jnp.concatenate(dests, axis=0).astype(jnp.int32)
    g_ref[...] = jnp.concatenate(gates, axis=0)


def _dest(gate_dense, rank_dense, pstart_col):
    e, n = gate_dense.shape
    t = min(ROUTE_T, n)
    ltri = jnp.asarray(np.tril(np.ones((e, e), np.float32), -1), jnp.bfloat16)
    whole = lambda i: (0, 0)
    tile = lambda i: (0, i)
    return pl.pallas_call(
        _dest_kernel,
        grid=(n // t,),
        in_specs=[pl.BlockSpec((e, t), tile), pl.BlockSpec((e, t), tile),
                  pl.BlockSpec((e, 1), whole), pl.BlockSpec((e, e), whole)],
        out_specs=[pl.BlockSpec((TOP_K, t), tile), pl.BlockSpec((TOP_K, t), tile)],
        out_shape=[jax.ShapeDtypeStruct((TOP_K, n), jnp.int32),
                   jax.ShapeDtypeStruct((TOP_K, n), jnp.float32)],
        compiler_params=_params(("parallel",)),
        name="moe_dest",
    )(gate_dense, rank_dense, pstart_col, ltri)


def _dispatch_kernel(dest_ref, padlo_ref, padn_ref, nu_ref, xp_ref, xs_ref, zero_ref, sem, zsem, *,
                     t_tile, nblk):
    i = pl.program_id(0)

    @pl.when(i == 0)
    def _():
        zero_ref[...] = jnp.zeros(zero_ref.shape, zero_ref.dtype)

        def for_each_pad_copy(e, fn):
            off = padlo_ref[e]
            cnt = padn_ref[e]
            for bit in reversed(range(MOE_BLOCK.bit_length() - 1)):
                size = 1 << bit
                on = (cnt & size) != 0

                @pl.when(on)
                def _():
                    fn(pltpu.make_async_copy(zero_ref.at[pl.ds(0, size)], xs_ref.at[pl.ds(off, size)], zsem))
                off = off + jnp.where(on, size, 0)

        def tail_copy(b):
            return pltpu.make_async_copy(zero_ref, xs_ref.at[pl.ds(b * MOE_BLOCK, MOE_BLOCK)], zsem)

        def start_pad(e, c):
            for_each_pad_copy(e, lambda cp: cp.start())
            return c

        def wait_pad(e, c):
            for_each_pad_copy(e, lambda cp: cp.wait())
            return c

        def start_tail(b, c):
            tail_copy(b).start()
            return c

        def wait_tail(b, c):
            tail_copy(b).wait()
            return c

        lax.fori_loop(0, N_EXPERTS, start_pad, 0)
        lax.fori_loop(nu_ref[0], nblk, start_tail, 0)
        lax.fori_loop(0, N_EXPERTS, wait_pad, 0)
        lax.fori_loop(nu_ref[0], nblk, wait_tail, 0)

    for t in range(t_tile):
        for k in range(TOP_K):
            pltpu.make_async_copy(xp_ref.at[t], xs_ref.at[dest_ref[k, t]], sem).start(priority=k % 2)
    for k in range(TOP_K):
        pltpu.make_async_copy(xp_ref, xs_ref.at[pl.ds(0, t_tile)], sem).wait()


def _dispatch(xp, dest, pad_lo, pad_n, n_used, n_rows):
    n = xp.shape[0]
    t = min(DISPATCH_T, n)
    nblk = n_rows // MOE_BLOCK
    smem = pl.BlockSpec(memory_space=pltpu.SMEM)
    return pl.pallas_call(
        functools.partial(_dispatch_kernel, t_tile=t, nblk=nblk),
        grid=(n // t,),
        in_specs=[
            pl.BlockSpec((TOP_K, t), lambda i: (0, i), memory_space=pltpu.SMEM),
            smem, smem, smem,
            pl.BlockSpec((t, ROW_SLAB, 128), lambda i: (i, 0, 0)),
        ],
        out_specs=pl.BlockSpec(memory_space=pl.ANY),
        out_shape=jax.ShapeDtypeStruct((n_rows, ROW_SLAB, 128), jnp.uint32),
        scratch_shapes=[pltpu.VMEM((MOE_BLOCK, ROW_SLAB, 128), jnp.uint32),
                        pltpu.SemaphoreType.DMA(()), pltpu.SemaphoreType.DMA(())],
        compiler_params=_params(("arbitrary",)),
        name="moe_dispatch",
    )(dest, pad_lo, pad_n, n_used, xp)


def _experts_kernel(be_ref, nu_ref, xs_ref, wg_ref, wu_ref, wd_ref, ys_ref, wgc_ref, wuc_ref, wdc_ref, h_ref,
                    *, nblk):
    b = pl.program_id(0)
    nu = nu_ref[0]

    def starts_run(j):
        return jnp.logical_or(j == 0, be_ref[j] != be_ref[jnp.maximum(j - 1, 0)])

    @pl.when(jnp.logical_and(b < nu, starts_run(jnp.minimum(b, nblk - 1))))
    def _():
        wgc_ref[...] = wg_ref[0, 0].astype(MXU_DTYPE)
        wuc_ref[...] = wu_ref[0, 0].astype(MXU_DTYPE)

    @pl.when(jnp.logical_and(jnp.logical_and(b >= 1, b <= nu), starts_run(jnp.maximum(b - 1, 0))))
    def _():
        wdc_ref[...] = wd_ref[0, 0].astype(MXU_DTYPE)

    def up_half():
        xb = jnp.concatenate(_unpack_rows(xs_ref, 0, MOE_BLOCK), axis=1).astype(MXU_DTYPE)
        g = _dot(xb, wgc_ref[...])
        u = _dot(xb, wuc_ref[...])
        return (_silu(g) * u).astype(MXU_DTYPE)

    def down_half(hmid):
        _pack_rows(_dot(hmid, wdc_ref[...]), ys_ref, 0, MOE_BLOCK)

    @pl.when(b == 0)
    def _():
        h_ref[...] = up_half()

    @pl.when(jnp.logical_and(b >= 1, b < nu))
    def _():
        h_prev = h_ref[...]
        h_new = up_half()
        down_half(h_prev)
        h_ref[...] = h_new

    @pl.when(b == nu)
    def _():
        down_half(h_ref[...])

    @pl.when(b > nu)
    def _():
        ys_ref[...] = jnp.zeros(ys_ref.shape, ys_ref.dtype)


def _experts(xs2d, block_e, n_used, w_gate, w_up, w_down, layer):
    d, hid = w_gate.shape[2], w_gate.shape[3]
    blk_rows = MOE_BLOCK * ROW_SLAB
    nblk = xs2d.shape[0] // blk_rows
    out_blk = lambda b, be, nu: (jnp.maximum(b - 1, 0), 0)
    used_blk = lambda b, be, nu: (jnp.minimum(b, nu[0] - 1), 0)
    expert = lambda b, be, nu: (layer, be[jnp.minimum(b, nblk - 1)], 0, 0)
    prev_expert = lambda b, be, nu: (layer, be[jnp.maximum(b - 1, 0)], 0, 0)
    grid_spec = pltpu.PrefetchScalarGridSpec(
        num_scalar_prefetch=2,
        grid=(nblk + 1,),
        in_specs=[
            pl.BlockSpec((blk_rows, 128), used_blk),
            pl.BlockSpec((1, 1, d, hid), expert),
            pl.BlockSpec((1, 1, d, hid), expert),
            pl.BlockSpec((1, 1, hid, d), prev_expert),
        ],
        out_specs=pl.BlockSpec((blk_rows, 128), out_blk),
        scratch_shapes=[pltpu.VMEM((d, hid), MXU_DTYPE), pltpu.VMEM((d, hid), MXU_DTYPE),
                        pltpu.VMEM((hid, d), MXU_DTYPE), pltpu.VMEM((MOE_BLOCK, hid), MXU_DTYPE)],
    )
    return pl.pallas_call(
        functools.partial(_experts_kernel, nblk=nblk),
        grid_spec=grid_spec,
        out_shape=jax.ShapeDtypeStruct(xs2d.shape, jnp.uint32),
        compiler_params=_params(("arbitrary",)),
        name="moe_experts",
    )(block_e, n_used, xs2d, w_gate, w_up, w_down)


def _combine_kernel(dest_ref, dnext_ref, g_ref, x_ref, wsg_ref, wsu_ref, wsd_ref, lg_ref, lb_ref, ys_ref,
                    ys2d_ref, o_ref, buf_ref, sem, *, t_tile):
    i = pl.program_id(0)
    last = pl.num_programs(0) - 1
    slot_rows = TOP_K * t_tile
    slot = lax.rem(i, 2)

    def gather(dref, s, unrolled):
        base = s * slot_rows

        def issue(t, carry):
            for k in range(TOP_K):
                r = pl.multiple_of((base + k * t_tile + t) * ROW_SLAB, ROW_SLAB)
                pltpu.make_async_copy(ys_ref.at[dref[k, t]], buf_ref.at[pl.ds(r, ROW_SLAB), :],
                                      sem).start(priority=k % 2)
            return carry

        if unrolled:
            for t in range(t_tile):
                issue(t, 0)
        else:
            lax.fori_loop(0, t_tile, issue, 0)

    def wait(s):
        n = t_tile * ROW_SLAB
        for k in range(TOP_K):
            r = pl.multiple_of((s * slot_rows + k * t_tile) * ROW_SLAB, ROW_SLAB)
            pltpu.make_async_copy(ys2d_ref.at[pl.ds(0, n), :], buf_ref.at[pl.ds(r, n), :], sem).wait()

    @pl.when(i == 0)
    def _():
        gather(dest_ref, 0, False)

    wait(slot)
    gather(dnext_ref, 1 - slot, True)

    x = x_ref[...]
    xb = x.astype(MXU_DTYPE)
    hmid = (_silu(_dot(xb, wsg_ref[...])) * _dot(xb, wsu_ref[...])).astype(MXU_DTYPE)
    z = DN_ALPHA * x + _dot(hmid, wsd_ref[...])

    gt = jnp.transpose(g_ref[...])
    cols = None
    for k in range(TOP_K):
        tiles = _unpack_rows(buf_ref, slot * slot_rows + k * t_tile, t_tile)
        gk = gt[:, k:k + 1]
        cols = [tl * gk for tl in tiles] if cols is None else [c + tl * gk for c, tl in zip(cols, tiles)]
    z = z + jnp.concatenate(cols, axis=1)
    o_ref[...] = _layer_norm(z, lg_ref[...], lb_ref[...])

    @pl.when(i == last)
    def _():
        wait(1 - slot)


def _combine(x, dest, gates, ys, ws_gate, ws_up, ws_down, ln_g, ln_b):
    n, d = x.shape
    t = min(COMBINE_T, n)
    steps = n // t
    whole = lambda i: (0, 0)
    tile = lambda i: (0, i)
    row = lambda i: (i, 0)
    nxt = lambda i: (0, jnp.minimum(i + 1, steps - 1))
    return pl.pallas_call(
        functools.partial(_combine_kernel, t_tile=t),
        grid=(steps,),
        in_specs=[
            pl.BlockSpec((TOP_K, t), tile, memory_space=pltpu.SMEM),
            pl.BlockSpec((TOP_K, t), nxt, memory_space=pltpu.SMEM),
            pl.BlockSpec((TOP_K, t), tile),
            pl.BlockSpec((t, d), row),
            pl.BlockSpec(ws_gate.shape, whole),
            pl.BlockSpec(ws_up.shape, whole),
            pl.BlockSpec(ws_down.shape, whole),
            pl.BlockSpec((1, d), whole),
            pl.BlockSpec((1, d), whole),
            pl.BlockSpec(memory_space=pl.ANY),
            pl.BlockSpec(memory_space=pl.ANY),
        ],
        out_specs=pl.BlockSpec((t, d), row),
        out_shape=jax.ShapeDtypeStruct((n, d), jnp.float32),
        scratch_shapes=[pltpu.VMEM((2 * TOP_K * t * ROW_SLAB, 128), jnp.uint32), pltpu.SemaphoreType.DMA(())],
        compiler_params=_params(("arbitrary",)),
        name="moe_combine",
    )(dest, dest, gates, x, ws_gate, ws_up, ws_down, ln_g, ln_b, ys, ys.reshape(-1, 128))


def _moe(x, xp, w_router_t, bias_col, eid_col, inv_perm, w_gate, w_up, w_down, layer, ws_gate, ws_up, ws_down,
         ln_g, ln_b):
    n, d = x.shape
    gate_dense, rank_dense, counts = _route(x, w_router_t, bias_col, eid_col)
    cnt = counts[:, 0].astype(jnp.int32)[inv_perm]
    padded = (cnt + MOE_BLOCK - 1) // MOE_BLOCK * MOE_BLOCK
    pends = jnp.cumsum(padded)
    pstarts = pends - padded
    nk = n * TOP_K
    n_rows = (nk + N_EXPERTS * (MOE_BLOCK - 1) + MOE_BLOCK - 1) // MOE_BLOCK * MOE_BLOCK
    nblk = n_rows // MOE_BLOCK
    blk_start = jnp.arange(nblk, dtype=jnp.int32) * MOE_BLOCK
    block_e = jnp.minimum(jnp.sum(blk_start[:, None] >= pends[None, :], axis=1), N_EXPERTS - 1).astype(jnp.int32)
    n_used = (pends[-1:] // MOE_BLOCK).astype(jnp.int32)
    perm = jnp.argsort(inv_perm)
    pstart_col = pstarts[perm].astype(jnp.float32)[:, None]
    dest, gates = _dest(gate_dense, rank_dense, pstart_col)
    xs = _dispatch(xp, dest, (pstarts + cnt).astype(jnp.int32), (padded - cnt).astype(jnp.int32), n_used, n_rows)
    ys = _experts(xs.reshape(n_rows * ROW_SLAB, 128), block_e, n_used, w_gate, w_up, w_down, layer)
    return _combine(x, dest, gates, ys.reshape(n_rows, ROW_SLAB, 128), ws_gate, ws_up, ws_down, ln_g, ln_b)


def kernel(x, mem, w_in_a, sinks, w_in_b, lambda_qk, subln_g, w_mem_kv, w_out, ln_g, ln_b, w_router,
           router_bias, w_gate, w_up, w_down, w_shared_gate, w_shared_up, w_shared_down):
    batch, seq, d = x.shape
    n = batch * seq
    cdt = MXU_DTYPE
    qa, kva = SWA_Q_HEADS * HEAD_DIM, SWA_KV_HEADS * HEAD_DIM
    qb, vb = DIFF_HEADS * 2 * HEAD_DIM, DIFF_HEADS * DIFF_V_DIM

    rows = np.arange(N_EXPERTS)
    perm = (rows % N_GROUPS) * PER_GROUP + rows // N_GROUPS
    inv_perm = jnp.asarray(np.argsort(perm), jnp.int32)
    eid_col = jnp.asarray(perm[:, None], jnp.int32)

    xf = x.reshape(n, d)
    kv_all = _mem_kv(mem.reshape(-1, d), w_mem_kv)
    mlen = mem.shape[1]
    kv_all = kv_all.reshape(DEPTH * batch, mlen, kv_all.shape[-1])

    def in_proj_of(i):
        j = i // 2
        if i % 2 == 0:
            w = w_in_a[j].astype(cdt)
            return ([w[:, :qa], w[:, qa:qa + kva], w[:, qa + kva:qa + 2 * kva], w[:, qa + 2 * kva:]],
                    (QK_SCALE, 1.0, 1.0, QK_SCALE))
        w = w_in_b[j].astype(cdt)
        return ([w[:, :qb], w[:, qb:2 * qb], w[:, 2 * qb:2 * qb + vb], w[:, 2 * qb + vb:]],
                (QK_SCALE * LOG2E, 1.0, 1.0, QK_SCALE))

    for i in range(DEPTH):
        j = i // 2
        q, k, v, qm = _project(xf, *in_proj_of(i))
        if i % 2 == 0:
            mix = _swa_attention(q, k, v, sinks[j], batch, seq)
        else:
            lam_init = 0.8 - 0.6 * math.exp(-0.3 * i)
            mix = _diff_attention(q, k, v, lambda_qk[j], subln_g[j][None, :], batch, seq, lam_init)
        kv_l = kv_all[i * batch:(i + 1) * batch]
        xf, xp = _out_proj(mix, qm, kv_l, w_out[i].astype(cdt), xf, ln_g[i, 0][None, :], ln_b[i, 0][None, :],
                           batch, seq)
        w_router_t = jnp.transpose(w_router[i])[perm]
        bias_col = router_bias[i][perm][:, None]
        xf = _moe(xf, xp, w_router_t, bias_col, eid_col, inv_perm,
                  w_gate, w_up, w_down, i,
                  w_shared_gate[i].astype(cdt), w_shared_up[i].astype(cdt), w_shared_down[i].astype(cdt),
                  ln_g[i, 1][None, :], ln_b[i, 1][None, :])
    return xf.reshape(batch, seq, d)
```

```python
import functools
import math

import jax
import jax.numpy as jnp
import numpy as np
from jax import lax
from jax.experimental import pallas as pl
from jax.experimental.pallas import tpu as pltpu

HEAD_DIM = 64
MEM_HEADS = 4
SWA_Q_HEADS = 12
SWA_KV_HEADS = 3
SWA_GROUP = SWA_Q_HEADS // SWA_KV_HEADS
WINDOW = 128
DIFF_HEADS = 6
DIFF_V_DIM = 2 * HEAD_DIM
N_EXPERTS = 64
TOP_K = 8
N_GROUPS = 8
PER_GROUP = N_EXPERTS // N_GROUPS
TOPK_GROUPS = 4
ROUTED_SCALE = 2.5
MOE_BLOCK = 1024
DEPTH = 4
DN_ALPHA = (2.0 * DEPTH) ** 0.25
LN_EPS = 1e-5
NEG = -1e30
QK_SCALE = HEAD_DIM ** -0.5

MXU_DTYPE = jnp.bfloat16

PROJ_TM = 512
DIFF_TQ = 1024
ROUTE_T = 512
DISPATCH_T = 256
COMBINE_T = 256
VMEM_LIMIT = 56 * 1024 * 1024


def _alibi_slopes(n):
    def pow2_slopes(m):
        start = 2.0 ** (-8.0 / m)
        return [start ** (i + 1) for i in range(m)]
    if (n & (n - 1)) == 0:
        return pow2_slopes(n)
    c = 2 ** int(math.floor(math.log2(n)))
    return pow2_slopes(c) + pow2_slopes(2 * c)[0::2][: n - c]


def _dot(a, b):
    return jnp.dot(a, b, preferred_element_type=jnp.float32)


def _dot_nt(a, b):
    return lax.dot_general(a, b, (((1,), (1,)), ((), ())), preferred_element_type=jnp.float32)


def _params(sem):
    return pltpu.CompilerParams(dimension_semantics=sem, vmem_limit_bytes=VMEM_LIMIT)


def _layer_norm(z, g, b):
    mu = jnp.mean(z, axis=-1, keepdims=True)
    zc = z - mu
    var = jnp.mean(zc * zc, axis=-1, keepdims=True)
    return zc * lax.rsqrt(var + LN_EPS) * g + b


def _silu(x):
    return x / (1.0 + jnp.exp(-x))


def _proj_kernel(*refs, scales):
    nw = len(scales)
    x_ref = refs[0]
    w_refs = refs[1:1 + nw]
    o_refs = refs[1 + nw:]
    xb = x_ref[...].astype(MXU_DTYPE)
    for w_ref, o_ref, sc in zip(w_refs, o_refs, scales):
        acc = _dot(xb, w_ref[...])
        if sc != 1.0:
            acc = acc * sc
        o_ref[...] = acc.astype(o_ref.dtype)


def _project(x, weights, scales):
    n, d = x.shape
    tm = min(PROJ_TM, n)
    in_specs = [pl.BlockSpec((tm, d), lambda i: (i, 0))]
    in_specs += [pl.BlockSpec(w.shape, lambda i: (0, 0)) for w in weights]
    out_specs = [pl.BlockSpec((tm, w.shape[1]), lambda i: (i, 0)) for w in weights]
    out_shape = [jax.ShapeDtypeStruct((n, w.shape[1]), MXU_DTYPE) for w in weights]
    return pl.pallas_call(
        functools.partial(_proj_kernel, scales=tuple(scales)),
        grid=(n // tm,),
        in_specs=in_specs,
        out_specs=out_specs,
        out_shape=out_shape,
        compiler_params=_params(("parallel",)),
        name="in_proj",
    )(x, *weights)


def _swa_kernel(sinks_ref, bias_ref, q_ref, kp_ref, kc_ref, vp_ref, vc_ref, o_ref):
    i = pl.program_id(1)
    q = q_ref[...]
    kp, kc, vp, vc = kp_ref[...], kc_ref[...], vp_ref[...], vc_ref[...]
    rows = SWA_GROUP * WINDOW
    head_of_row = lax.broadcasted_iota(jnp.int32, (rows, 1), 0) // WINDOW
    vpad = jnp.concatenate([jnp.zeros((WINDOW, HEAD_DIM), MXU_DTYPE),
                            jnp.ones((WINDOW, 2 * HEAD_DIM), MXU_DTYPE)], axis=1)
    outs = []
    for kv in range(SWA_KV_HEADS):
        ks = slice(kv * HEAD_DIM, (kv + 1) * HEAD_DIM)
        kph, kch, vph, vch = kp[:, ks], kc[:, ks], vp[:, ks], vc[:, ks]
        h0 = kv * SWA_GROUP
        qg = jnp.concatenate([q[:, (h0 + g) * HEAD_DIM:(h0 + g + 1) * HEAD_DIM] for g in range(SWA_GROUP)],
                             axis=0)
        sink = jnp.zeros((rows, 1), jnp.float32) + sinks_ref[h0]
        for g in range(1, SWA_GROUP):
            sink = jnp.where(head_of_row == g, sinks_ref[h0 + g], sink)
        sp = jnp.where(i > 0, _dot_nt(qg, kph) + bias_ref[kv, 0], NEG)
        sc = _dot_nt(qg, kch) + bias_ref[kv, 1]
        m = jnp.maximum(jnp.maximum(jnp.max(sp, axis=-1, keepdims=True),
                                    jnp.max(sc, axis=-1, keepdims=True)), sink)
        pp = jnp.exp(sp - m).astype(MXU_DTYPE)
        pc = jnp.exp(sc - m).astype(MXU_DTYPE)
        ov = _dot(pp, jnp.concatenate([vph, vpad], axis=1)) + _dot(pc, jnp.concatenate([vch, vpad], axis=1))
        denom = ov[:, 2 * HEAD_DIM:3 * HEAD_DIM] + jnp.exp(sink - m)
        o = ov[:, :HEAD_DIM] / denom
        outs += [o[g * WINDOW:(g + 1) * WINDOW] for g in range(SWA_GROUP)]
    o_ref[...] = jnp.concatenate(outs, axis=-1).astype(o_ref.dtype)


def _swa_bias():
    slopes = np.asarray(_alibi_slopes(SWA_Q_HEADS), np.float64).reshape(SWA_KV_HEADS, SWA_GROUP)
    qi = np.arange(WINDOW)[:, None]
    kj = np.arange(WINDOW)[None, :]
    dist_c = qi - kj
    dist_p = dist_c + WINDOW
    out = np.empty((SWA_KV_HEADS, 2, SWA_GROUP, WINDOW, WINDOW), np.float32)
    for kv in range(SWA_KV_HEADS):
        for g in range(SWA_GROUP):
            out[kv, 0, g] = np.where(dist_p < WINDOW, -slopes[kv, g] * dist_p, NEG)
            out[kv, 1, g] = np.where(dist_c >= 0, -slopes[kv, g] * dist_c, NEG)
    return jnp.asarray(out.reshape(SWA_KV_HEADS, 2, SWA_GROUP * WINDOW, WINDOW))


def _swa_attention(q, k, v, sinks, batch, seq):
    n = batch * seq
    nb = seq // WINDOW
    qw, kw = q.shape[1], k.shape[1]
    cur = lambda b, i: (b * nb + i, 0)
    prev = lambda b, i: (b * nb + jnp.maximum(i - 1, 0), 0)
    bias = _swa_bias()
    return pl.pallas_call(
        _swa_kernel,
        grid=(batch, nb),
        in_specs=[
            pl.BlockSpec(memory_space=pltpu.SMEM),
            pl.BlockSpec(bias.shape, lambda b, i: (0, 0, 0, 0)),
            pl.BlockSpec((WINDOW, qw), cur),
            pl.BlockSpec((WINDOW, kw), prev),
            pl.BlockSpec((WINDOW, kw), cur),
            pl.BlockSpec((WINDOW, kw), prev),
            pl.BlockSpec((WINDOW, kw), cur),
        ],
        out_specs=pl.BlockSpec((WINDOW, qw), cur),
        out_shape=jax.ShapeDtypeStruct((n, qw), MXU_DTYPE),
        compiler_params=_params(("parallel", "parallel")),
        name="swa_attention",
    )(sinks, bias, q, k, k, v, v)


LOG2E = math.log2(math.e)
_BIAS_PIECES = 3


def _bf16_pieces(c):
    out, rest = [], np.float64(c)
    for _ in range(_BIAS_PIECES):
        piece = np.asarray(rest, np.float32).astype(jnp.bfloat16).astype(np.float64)
        out.append(float(piece))
        rest = rest - piece
    return out


def _alibi_templates(tq):
    idx = np.arange(tq)
    hi, lo = (idx // 256) * 256, idx % 256
    hw = 2 * HEAD_DIM
    qt = np.zeros((DIFF_HEADS, 2, tq, hw), np.float32)
    kt = np.zeros((DIFF_HEADS, 2, tq, hw), np.float32)
    for h, slope in enumerate(_alibi_slopes(DIFF_HEADS)):
        for c in range(2):
            base = HEAD_DIM if c == 0 else 0
            for p, piece in enumerate(_bf16_pieces(slope * LOG2E)):
                qt[h, c, :, base + p] = -hi
                kt[h, c, :, base + p] = piece
                qt[h, c, :, base + _BIAS_PIECES + p] = -lo
                kt[h, c, :, base + _BIAS_PIECES + p] = piece
                qt[h, c, :, base + 2 * _BIAS_PIECES + p] = piece
                kt[h, c, :, base + 2 * _BIAS_PIECES + p] = hi
                qt[h, c, :, base + 3 * _BIAS_PIECES + p] = piece
                kt[h, c, :, base + 3 * _BIAS_PIECES + p] = lo
    return jnp.asarray(qt, jnp.bfloat16), jnp.asarray(kt, jnp.bfloat16)


def _diff_kernel(slopes_ref, q_ref, k_ref, v_ref, qt_ref, kt_ref, ones_ref, dmask_ref, lam_ref, g_ref,
                 o_ref, qa_ref, m_ref, acc_ref, *, tq, lam_init):
    h = pl.program_id(1)
    qi = pl.program_id(2)
    slope2 = slopes_ref[h]
    hw = 2 * HEAD_DIM
    lane = lax.broadcasted_iota(jnp.int32, (1, hw), 1)
    first = lane < HEAD_DIM
    q = q_ref[...]
    qa_ref[0] = jnp.where(first, q, qt_ref[0, 0])
    qa_ref[1] = jnp.where(first, qt_ref[0, 1], q)
    m_ref[...] = jnp.full(m_ref.shape, NEG, jnp.float32)
    acc_ref[...] = jnp.zeros(acc_ref.shape, jnp.float32)
    ones = ones_ref[...]

    def softmax_pv(c, u, vaug, shift):
        m_old = m_ref[c]
        m_new = jnp.maximum(m_old, jnp.max(u, axis=-1, keepdims=True) + shift)
        p = jnp.exp2(u - (m_new - shift)).astype(MXU_DTYPE)
        alpha = jnp.exp2(m_old - m_new)
        acc_ref[c] = alpha * acc_ref[c] + _dot(p, vaug)
        m_ref[c] = m_new

    def process(ki, tiles, diagonal):
        vaugs, us, shifts = [], [], []
        for s in range(tiles):
            start = pl.multiple_of((ki + s) * tq, tq)
            k = k_ref[pl.ds(start, tq), :]
            v = v_ref[pl.ds(start, tq), :]
            vaugs.append(jnp.concatenate([v, ones], axis=1))
            kas = (jnp.where(first, k, kt_ref[0, 0]), jnp.where(first, kt_ref[0, 1], k))
            us.append([_dot_nt(qa_ref[c], kas[c]) for c in range(2)])
            shifts.append(slope2 * ((ki + s - qi) * tq).astype(jnp.float32))
        if diagonal:
            us = [[u + dmask_ref[...] for u in pair] for pair in us]
        for s in range(tiles):
            for c in range(2):
                softmax_pv(c, us[s][c], vaugs[s], shifts[s])

    def two_tiles(it, carry):
        process(2 * it, 2, False)
        return carry

    def one_tile(ki, carry):
        process(ki, 1, False)
        return carry

    pairs = qi // 2
    lax.fori_loop(0, pairs, two_tiles, 0)
    lax.fori_loop(2 * pairs, qi, one_tile, 0)
    process(qi, 1, True)

    lp = lam_ref[...]
    lam = (jnp.exp(jnp.sum(lp[0:1] * lp[1:2], axis=-1, keepdims=True))
           - jnp.exp(jnp.sum(lp[2:3] * lp[3:4], axis=-1, keepdims=True)) + lam_init)
    a0, a1 = acc_ref[0], acc_ref[1]
    o = a0[:, :hw] / a0[:, hw:] - lam * (a1[:, :hw] / a1[:, hw:])
    ms = jnp.mean(o * o, axis=-1, keepdims=True)
    o = o * lax.rsqrt(ms + LN_EPS) * g_ref[...] * (1.0 - lam_init)
    o_ref[...] = o.astype(o_ref.dtype)


def _diff_attention(q, k, v, lam_qk, subln_g, batch, seq, lam_init):
    n = batch * seq
    tq = min(DIFF_TQ, seq)
    nq = seq // tq
    hw = 2 * HEAD_DIM
    slopes2 = np.asarray(_alibi_slopes(DIFF_HEADS), np.float64) * LOG2E
    qt, kt = _alibi_templates(tq)
    ones_col = np.ones((tq, hw), np.float32)
    rel = np.arange(tq)[None, :] - np.arange(tq)[:, None]
    dmask = jnp.asarray(np.where(rel <= 0, 0.0, NEG).astype(np.float32))
    kernel = functools.partial(_diff_kernel, tq=tq, lam_init=lam_init)
    fixed = lambda b, h, i: (0, 0)
    per_head = lambda b, h, i: (h, 0, 0, 0)
    return pl.pallas_call(
        kernel,
        grid=(batch, DIFF_HEADS, nq),
        in_specs=[
            pl.BlockSpec(memory_space=pltpu.SMEM),
            pl.BlockSpec((tq, hw), lambda b, h, i: (b * nq + i, h)),
            pl.BlockSpec((seq, hw), lambda b, h, i: (b, h)),
            pl.BlockSpec((seq, hw), lambda b, h, i: (b, h)),
            pl.BlockSpec((1, 2, tq, hw), per_head),
            pl.BlockSpec((1, 2, tq, hw), per_head),
            pl.BlockSpec((tq, hw), fixed),
            pl.BlockSpec((tq, tq), fixed),
            pl.BlockSpec(lam_qk.shape, fixed),
            pl.BlockSpec(subln_g.shape, fixed),
        ],
        out_specs=pl.BlockSpec((tq, hw), lambda b, h, i: (b * nq + i, h)),
        out_shape=jax.ShapeDtypeStruct((n, DIFF_HEADS * hw), MXU_DTYPE),
        scratch_shapes=[
            pltpu.VMEM((2, tq, hw), MXU_DTYPE),
            pltpu.VMEM((2, tq, 1), jnp.float32),
            pltpu.VMEM((2, tq, 2 * hw), jnp.float32),
        ],
        compiler_params=_params(("parallel", "parallel", "arbitrary")),
        name="diff_attention",
    )(jnp.asarray(slopes2, jnp.float32), q, k, v, qt, kt, jnp.asarray(ones_col, MXU_DTYPE), dmask,
      lam_qk, subln_g)


def _memkv_kernel(mem_ref, w_ref, o_ref):
    o_ref[0] = _dot(mem_ref[...].astype(MXU_DTYPE), w_ref[0].astype(MXU_DTYPE)).astype(o_ref.dtype)


def _mem_kv(mem2d, w_mem_kv):
    depth, d, w = w_mem_kv.shape
    rows = mem2d.shape[0]
    return pl.pallas_call(
        _memkv_kernel,
        grid=(depth,),
        in_specs=[pl.BlockSpec((rows, d), lambda l: (0, 0)),
                  pl.BlockSpec((1, d, w), lambda l: (l, 0, 0))],
        out_specs=pl.BlockSpec((1, rows, w), lambda l: (l, 0, 0)),
        out_shape=jax.ShapeDtypeStruct((depth, rows, w), MXU_DTYPE),
        compiler_params=_params(("parallel",)),
        name="mem_kv_proj",
    )(mem2d, w_mem_kv)


ROW_SLAB = 4
_HI16 = 0xFFFF0000


def _bf16_bits_hi(x):
    b = lax.bitcast_convert_type(x, jnp.uint32)
    b = b + jnp.uint32(0x7FFF) + ((b >> 16) & jnp.uint32(1))
    return b & jnp.uint32(_HI16)


def _pack_rows(z, ref, base, t):
    half = z.shape[1] // 2
    words = _bf16_bits_hi(z[:, half:]) | (_bf16_bits_hi(z[:, :half]) >> 16)
    for s in range(ROW_SLAB):
        ref[pl.ds(base * ROW_SLAB + s, t, stride=ROW_SLAB), :] = words[:, s * 128:(s + 1) * 128]


def _unpack_rows(ref, base, t):
    lo, hi = [], []
    for s in range(ROW_SLAB):
        w = ref[pl.ds(base * ROW_SLAB + s, t, stride=ROW_SLAB), :]
        lo.append(lax.bitcast_convert_type(w << 16, jnp.float32))
        hi.append(lax.bitcast_convert_type(w & jnp.uint32(_HI16), jnp.float32))
    return lo + hi


def _outproj_kernel(mix_ref, qm_ref, kv_ref, wo_ref, x_ref, g_ref, b_ref, o_ref, op_ref):
    qm = qm_ref[...]
    kvm = kv_ref[0]
    mw = MEM_HEADS * HEAD_DIM
    heads = [slice(h * HEAD_DIM, (h + 1) * HEAD_DIM) for h in range(MEM_HEADS)]
    scores = [_dot_nt(qm[:, hs], kvm[:, hs]) for hs in heads]
    tops = [jnp.max(s, axis=-1, keepdims=True) for s in scores]
    probs = [jnp.exp(s - m) for s, m in zip(scores, tops)]
    sums = [jnp.sum(p, axis=-1, keepdims=True) for p in probs]
    outs = [_dot(p.astype(MXU_DTYPE), kvm[:, mw + hs.start: mw + hs.stop]) / l
            for p, hs, l in zip(probs, heads, sums)]
    mem_out = jnp.concatenate(outs, axis=-1).astype(MXU_DTYPE)
    mixw = mix_ref.shape[1]
    y = _dot(mix_ref[...], wo_ref[0:mixw, :]) + _dot(mem_out, wo_ref[mixw:, :])
    z = DN_ALPHA * x_ref[...] + y
    zn = _layer_norm(z, g_ref[...], b_ref[...])
    o_ref[...] = zn
    _pack_rows(zn, op_ref, 0, zn.shape[0])


def _out_proj(mix, qm, kv_l, w_out, x, ln_g, ln_b, batch, seq):
    n, d = x.shape
    tm = min(PROJ_TM, seq)
    per_b = seq // tm
    mlen, kvw = kv_l.shape[1], kv_l.shape[2]
    row = lambda i: (i, 0)
    whole = lambda i: (0, 0)
    xn, xp = pl.pallas_call(
        _outproj_kernel,
        grid=(n // tm,),
        in_specs=[
            pl.BlockSpec((tm, mix.shape[1]), row),
            pl.BlockSpec((tm, qm.shape[1]), row),
            pl.BlockSpec((1, mlen, kvw), lambda i: (i // per_b, 0, 0)),
            pl.BlockSpec(w_out.shape, whole),
            pl.BlockSpec((tm, d), row),
            pl.BlockSpec((1, d), whole),
            pl.BlockSpec((1, d), whole),
        ],
        out_specs=[pl.BlockSpec((tm, d), row), pl.BlockSpec((tm * ROW_SLAB, 128), row)],
        out_shape=[jax.ShapeDtypeStruct((n, d), jnp.float32),
                   jax.ShapeDtypeStruct((n * ROW_SLAB, 128), jnp.uint32)],
        compiler_params=_params(("parallel",)),
        name="out_proj_ln",
    )(mix, qm, kv_l, w_out, x, ln_g, ln_b)
    return xn, xp.reshape(n, ROW_SLAB, 128)


def _split_hi_lo(a):
    hi = a.astype(jnp.bfloat16)
    lo = (a - hi.astype(jnp.float32)).astype(jnp.bfloat16)
    return hi, lo


def _route_kernel(x_ref, w_ref, bias_ref, eid_ref, tri_ref, gate_ref, rank_ref, cnt_ref, carry_ref):
    i = pl.program_id(0)

    @pl.when(i == 0)
    def _():
        carry_ref[...] = jnp.zeros(carry_ref.shape, jnp.float32)

    xh, xl = _split_hi_lo(x_ref[...])
    wh, wl = _split_hi_lo(w_ref[...])
    logits = _dot_nt(wh, xh) + _dot_nt(wh, xl) + _dot_nt(wl, xh)
    aff = 1.0 / (1.0 + jnp.exp(-logits))
    sel = aff + bias_ref[...]
    t = sel.shape[1]

    sel3 = sel.reshape(PER_GROUP, N_GROUPS, t)
    jidx = lax.broadcasted_iota(jnp.int32, (PER_GROUP, N_GROUPS, t), 0)
    m1 = jnp.max(sel3, axis=0, keepdims=True)
    first = jnp.min(jnp.where(sel3 == m1, jidx, PER_GROUP), axis=0, keepdims=True)
    m2 = jnp.max(jnp.where(jidx == first, -jnp.inf, sel3), axis=0, keepdims=True)
    gscore = (m1 + m2)[0]

    gid = lax.broadcasted_iota(jnp.int32, (N_GROUPS, t), 0)
    grank = jnp.zeros((N_GROUPS, t), jnp.int32)
    for g2 in range(N_GROUPS):
        other = gscore[g2:g2 + 1, :]
        beats = jnp.logical_or(other > gscore, jnp.logical_and(other == gscore, g2 < gid))
        grank = grank + beats.astype(jnp.int32)
    gmask = grank < TOPK_GROUPS
    emask = jnp.broadcast_to(gmask[None], (PER_GROUP, N_GROUPS, t)).reshape(N_EXPERTS, t)
    val = jnp.where(emask, sel, -jnp.inf)

    eid = eid_ref[...]
    chosen = jnp.zeros((N_EXPERTS, t), jnp.bool_)
    for _ in range(TOP_K):
        best = jnp.max(val, axis=0, keepdims=True)
        pick = jnp.min(jnp.where(val == best, eid, N_EXPERTS), axis=0, keepdims=True)
        hit = eid == pick
        chosen = jnp.logical_or(chosen, hit)
        val = jnp.where(hit, -jnp.inf, val)
    picked = jnp.where(chosen, aff, 0.0)
    gate_ref[...] = picked / jnp.sum(picked, axis=0, keepdims=True) * ROUTED_SCALE

    ind = chosen.astype(jnp.bfloat16)
    prefix = _dot(ind, tri_ref[...])
    carry = carry_ref[...]
    rank_ref[...] = jnp.where(chosen, prefix + carry, -1.0)
    carry = carry + jnp.sum(chosen.astype(jnp.float32), axis=1, keepdims=True)
    carry_ref[...] = carry
    cnt_ref[...] = carry


def _route(x, w_router_t, bias_col, eid_col):
    n, d = x.shape
    t = min(ROUTE_T, n)
    tri = jnp.asarray(np.triu(np.ones((t, t), np.float32), 1), jnp.bfloat16)
    whole = lambda i: (0, 0)
    return pl.pallas_call(
        _route_kernel,
        grid=(n // t,),
        in_specs=[
            pl.BlockSpec((t, d), lambda i: (i, 0)),
            pl.BlockSpec((N_EXPERTS, d), whole),
            pl.BlockSpec((N_EXPERTS, 1), whole),
            pl.BlockSpec((N_EXPERTS, 1), whole),
            pl.BlockSpec((t, t), whole),
        ],
        out_specs=[
            pl.BlockSpec((N_EXPERTS, t), lambda i: (0, i)),
            pl.BlockSpec((N_EXPERTS, t), lambda i: (0, i)),
            pl.BlockSpec((N_EXPERTS, 1), whole),
        ],
        out_shape=[
            jax.ShapeDtypeStruct((N_EXPERTS, n), jnp.float32),
            jax.ShapeDtypeStruct((N_EXPERTS, n), jnp.float32),
            jax.ShapeDtypeStruct((N_EXPERTS, 1), jnp.float32),
        ],
        scratch_shapes=[pltpu.VMEM((N_EXPERTS, 1), jnp.float32)],
        compiler_params=_params(("arbitrary",)),
        name="moe_route",
    )(x, w_router_t, bias_col, eid_col, tri)


def _dest_kernel(gate_ref, rank_ref, pstart_ref, ltri_ref, dest_ref, g_ref):
    rank = rank_ref[...]
    chosen = rank >= 0.0
    slot = _dot(ltri_ref[...], chosen.astype(jnp.bfloat16))
    row = rank + pstart_ref[...]
    gate = gate_ref[...]
    dests, gates = [], []
    for k in range(TOP_K):
        hit = jnp.logical_and(chosen, slot == float(k))
        dests.append(jnp.sum(jnp.where(hit, row, 0.0), axis=0, keepdims=True))
        gates.append(jnp.sum(jnp.where(hit, gate, 0.0), axis=0, keepdims=True))
    dest_ref[...] = jnp.concatenate(dests, axis=0).astype(jnp.int32)
    g_ref[...] = jnp.concatenate(gates, axis=0)


def _dest(gate_dense, rank_dense, pstart_col):
    e, n = gate_dense.shape
    t = min(ROUTE_T, n)
    ltri = jnp.asarray(np.tril(np.ones((e, e), np.float32), -1), jnp.bfloat16)
    whole = lambda i: (0, 0)
    tile = lambda i: (0, i)
    return pl.pallas_call(
        _dest_kernel,
        grid=(n // t,),
        in_specs=[pl.BlockSpec((e, t), tile), pl.BlockSpec((e, t), tile),
                  pl.BlockSpec((e, 1), whole), pl.BlockSpec((e, e), whole)],
        out_specs=[pl.BlockSpec((TOP_K, t), tile), pl.BlockSpec((TOP_K, t), tile)],
        out_shape=[jax.ShapeDtypeStruct((TOP_K, n), jnp.int32),
                   jax.ShapeDtypeStruct((TOP_K, n), jnp.float32)],
        compiler_params=_params(("parallel",)),
        name="moe_dest",
    )(gate_dense, rank_dense, pstart_col, ltri)


def _dispatch_kernel(dest_ref, padlo_ref, padn_ref, nu_ref, xp_ref, xph_ref, xs_ref, zero_ref, sem, zsem, *,
                     t_tile, nblk):
    i = pl.program_id(0)

    @pl.when(i == 0)
    def _():
        zero_ref[...] = jnp.zeros(zero_ref.shape, zero_ref.dtype)

        def for_each_pad_copy(e, fn):
            off = padlo_ref[e]
            cnt = padn_ref[e]
            for bit in reversed(range(MOE_BLOCK.bit_length() - 1)):
                size = 1 << bit
                on = (cnt & size) != 0

                @pl.when(on)
                def _():
                    fn(pltpu.make_async_copy(zero_ref.at[pl.ds(0, size)], xs_ref.at[pl.ds(off, size)], zsem))
                off = off + jnp.where(on, size, 0)

        def tail_copy(b):
            return pltpu.make_async_copy(zero_ref, xs_ref.at[pl.ds(b * MOE_BLOCK, MOE_BLOCK)], zsem)

        def start_pad(e, c):
            for_each_pad_copy(e, lambda cp: cp.start())
            return c

        def wait_pad(e, c):
            for_each_pad_copy(e, lambda cp: cp.wait())
            return c

        def start_tail(b, c):
            tail_copy(b).start()
            return c

        def wait_tail(b, c):
            tail_copy(b).wait()
            return c

        lax.fori_loop(0, N_EXPERTS, start_pad, 0)
        lax.fori_loop(nu_ref[0], nblk, start_tail, 0)
        lax.fori_loop(0, N_EXPERTS, wait_pad, 0)
        lax.fori_loop(nu_ref[0], nblk, wait_tail, 0)

    for t in range(t_tile):
        for k in range(TOP_K):
            src = xp_ref.at[t] if k % 2 == 0 else xph_ref.at[i * t_tile + t]
            pltpu.make_async_copy(src, xs_ref.at[dest_ref[k, t]], sem).start(priority=(k // 2) % 2)
    for k in range(TOP_K):
        pltpu.make_async_copy(xp_ref, xs_ref.at[pl.ds(0, t_tile)], sem).wait()


def _dispatch(xp, dest, pad_lo, pad_n, n_used, n_rows):
    n = xp.shape[0]
    t = min(DISPATCH_T, n)
    nblk = n_rows // MOE_BLOCK
    smem = pl.BlockSpec(memory_space=pltpu.SMEM)
    return pl.pallas_call(
        functools.partial(_dispatch_kernel, t_tile=t, nblk=nblk),
        grid=(n // t,),
        in_specs=[
            pl.BlockSpec((TOP_K, t), lambda i: (0, i), memory_space=pltpu.SMEM),
            smem, smem, smem,
            pl.BlockSpec((t, ROW_SLAB, 128), lambda i: (i, 0, 0)),
            pl.BlockSpec(memory_space=pl.ANY),
        ],
        out_specs=pl.BlockSpec(memory_space=pl.ANY),
        out_shape=jax.ShapeDtypeStruct((n_rows, ROW_SLAB, 128), jnp.uint32),
        scratch_shapes=[pltpu.VMEM((MOE_BLOCK, ROW_SLAB, 128), jnp.uint32),
                        pltpu.SemaphoreType.DMA(()), pltpu.SemaphoreType.DMA(())],
        compiler_params=_params(("arbitrary",)),
        name="moe_dispatch",
    )(dest, pad_lo, pad_n, n_used, xp, xp)


def _experts_kernel(be_ref, nu_ref, xs_ref, wg_ref, wu_ref, wd_ref, ys_ref, wgc_ref, wuc_ref, wdc_ref, h_ref,
                    *, nblk):
    b = pl.program_id(0)
    nu = nu_ref[0]

    def starts_run(j):
        return jnp.logical_or(j == 0, be_ref[j] != be_ref[jnp.maximum(j - 1, 0)])

    @pl.when(jnp.logical_and(b < nu, starts_run(jnp.minimum(b, nblk - 1))))
    def _():
        wgc_ref[...] = wg_ref[0, 0].astype(MXU_DTYPE)
        wuc_ref[...] = wu_ref[0, 0].astype(MXU_DTYPE)

    @pl.when(jnp.logical_and(jnp.logical_and(b >= 1, b <= nu), starts_run(jnp.maximum(b - 1, 0))))
    def _():
        wdc_ref[...] = wd_ref[0, 0].astype(MXU_DTYPE)

    def up_half():
        xb = jnp.concatenate(_unpack_rows(xs_ref, 0, MOE_BLOCK), axis=1).astype(MXU_DTYPE)
        g = _dot(xb, wgc_ref[...])
        u = _dot(xb, wuc_ref[...])
        return (_silu(g) * u).astype(MXU_DTYPE)

    def down_half(hmid):
        _pack_rows(_dot(hmid, wdc_ref[...]), ys_ref, 0, MOE_BLOCK)

    @pl.when(b == 0)
    def _():
        h_ref[...] = up_half()

    @pl.when(jnp.logical_and(b >= 1, b < nu))
    def _():
        h_prev = h_ref[...]
        h_new = up_half()
        down_half(h_prev)
        h_ref[...] = h_new

    @pl.when(b == nu)
    def _():
        down_half(h_ref[...])

    @pl.when(b > nu)
    def _():
        ys_ref[...] = jnp.zeros(ys_ref.shape, ys_ref.dtype)


def _experts(xs2d, block_e, n_used, w_gate, w_up, w_down, layer):
    d, hid = w_gate.shape[2], w_gate.shape[3]
    blk_rows = MOE_BLOCK * ROW_SLAB
    nblk = xs2d.shape[0] // blk_rows
    out_blk = lambda b, be, nu: (jnp.maximum(b - 1, 0), 0)
    used_blk = lambda b, be, nu: (jnp.minimum(b, nu[0] - 1), 0)
    expert = lambda b, be, nu: (layer, be[jnp.minimum(b, nblk - 1)], 0, 0)
    prev_expert = lambda b, be, nu: (layer, be[jnp.maximum(b - 1, 0)], 0, 0)
    grid_spec = pltpu.PrefetchScalarGridSpec(
        num_scalar_prefetch=2,
        grid=(nblk + 1,),
        in_specs=[
            pl.BlockSpec((blk_rows, 128), used_blk),
            pl.BlockSpec((1, 1, d, hid), expert),
            pl.BlockSpec((1, 1, d, hid), expert),
            pl.BlockSpec((1, 1, hid, d), prev_expert),
        ],
        out_specs=pl.BlockSpec((blk_rows, 128), out_blk),
        scratch_shapes=[pltpu.VMEM((d, hid), MXU_DTYPE), pltpu.VMEM((d, hid), MXU_DTYPE),
                        pltpu.VMEM((hid, d), MXU_DTYPE), pltpu.VMEM((MOE_BLOCK, hid), MXU_DTYPE)],
    )
    return pl.pallas_call(
        functools.partial(_experts_kernel, nblk=nblk),
        grid_spec=grid_spec,
        out_shape=jax.ShapeDtypeStruct(xs2d.shape, jnp.uint32),
        compiler_params=_params(("arbitrary",)),
        name="moe_experts",
    )(block_e, n_used, xs2d, w_gate, w_up, w_down)


def _combine_kernel(dest_ref, dnext_ref, g_ref, x_ref, wsg_ref, wsu_ref, wsd_ref, lg_ref, lb_ref, ys_ref,
                    ys2d_ref, o_ref, buf_ref, sem, *, t_tile):
    i = pl.program_id(0)
    last = pl.num_programs(0) - 1
    slot_rows = TOP_K * t_tile
    slot = lax.rem(i, 2)

    def gather(dref, s, unrolled):
        base = s * slot_rows

        def issue(t, carry):
            for k in range(TOP_K):
                r = pl.multiple_of((base + k * t_tile + t) * ROW_SLAB, ROW_SLAB)
                pltpu.make_async_copy(ys_ref.at[dref[k, t]], buf_ref.at[pl.ds(r, ROW_SLAB), :],
                                      sem).start(priority=k % 2)
            return carry

        if unrolled:
            for t in range(t_tile):
                issue(t, 0)
        else:
            lax.fori_loop(0, t_tile, issue, 0)

    def wait(s):
        n = t_tile * ROW_SLAB
        for k in range(TOP_K):
            r = pl.multiple_of((s * slot_rows + k * t_tile) * ROW_SLAB, ROW_SLAB)
            pltpu.make_async_copy(ys2d_ref.at[pl.ds(0, n), :], buf_ref.at[pl.ds(r, n), :], sem).wait()

    @pl.when(i == 0)
    def _():
        gather(dest_ref, 0, False)

    wait(slot)
    gather(dnext_ref, 1 - slot, True)

    x = x_ref[...]
    xb = x.astype(MXU_DTYPE)
    hmid = (_silu(_dot(xb, wsg_ref[...])) * _dot(xb, wsu_ref[...])).astype(MXU_DTYPE)
    z = DN_ALPHA * x + _dot(hmid, wsd_ref[...])

    gt = jnp.transpose(g_ref[...])
    cols = None
    for k in range(TOP_K):
        tiles = _unpack_rows(buf_ref, slot * slot_rows + k * t_tile, t_tile)
        gk = gt[:, k:k + 1]
        cols = [tl * gk for tl in tiles] if cols is None else [c + tl * gk for c, tl in zip(cols, tiles)]
    z = z + jnp.concatenate(cols, axis=1)
    o_ref[...] = _layer_norm(z, lg_ref[...], lb_ref[...])

    @pl.when(i == last)
    def _():
        wait(1 - slot)


def _combine(x, dest, gates, ys, ws_gate, ws_up, ws_down, ln_g, ln_b):
    n, d = x.shape
    t = min(COMBINE_T, n)
    steps = n // t
    whole = lambda i: (0, 0)
    tile = lambda i: (0, i)
    row = lambda i: (i, 0)
    nxt = lambda i: (0, jnp.minimum(i + 1, steps - 1))
    return pl.pallas_call(
        functools.partial(_combine_kernel, t_tile=t),
        grid=(steps,),
        in_specs=[
            pl.BlockSpec((TOP_K, t), tile, memory_space=pltpu.SMEM),
            pl.BlockSpec((TOP_K, t), nxt, memory_space=pltpu.SMEM),
            pl.BlockSpec((TOP_K, t), tile),
            pl.BlockSpec((t, d), row),
            pl.BlockSpec(ws_gate.shape, whole),
            pl.BlockSpec(ws_up.shape, whole),
            pl.BlockSpec(ws_down.shape, whole),
            pl.BlockSpec((1, d), whole),
            pl.BlockSpec((1, d), whole),
            pl.BlockSpec(memory_space=pl.ANY),
            pl.BlockSpec(memory_space=pl.ANY),
        ],
        out_specs=pl.BlockSpec((t, d), row),
        out_shape=jax.ShapeDtypeStruct((n, d), jnp.float32),
        scratch_shapes=[pltpu.VMEM((2 * TOP_K * t * ROW_SLAB, 128), jnp.uint32), pltpu.SemaphoreType.DMA(())],
        compiler_params=_params(("arbitrary",)),
        name="moe_combine",
    )(dest, dest, gates, x, ws_gate, ws_up, ws_down, ln_g, ln_b, ys, ys.reshape(-1, 128))


def _moe(x, xp, w_router_t, bias_col, eid_col, inv_perm, w_gate, w_up, w_down, layer, ws_gate, ws_up, ws_down,
         ln_g, ln_b):
    n, d = x.shape
    gate_dense, rank_dense, counts = _route(x, w_router_t, bias_col, eid_col)
    cnt = counts[:, 0].astype(jnp.int32)[inv_perm]
    padded = (cnt + MOE_BLOCK - 1) // MOE_BLOCK * MOE_BLOCK
    pends = jnp.cumsum(padded)
    pstarts = pends - padded
    nk = n * TOP_K
    n_rows = (nk + N_EXPERTS * (MOE_BLOCK - 1) + MOE_BLOCK - 1) // MOE_BLOCK * MOE_BLOCK
    nblk = n_rows // MOE_BLOCK
    blk_start = jnp.arange(nblk, dtype=jnp.int32) * MOE_BLOCK
    block_e = jnp.minimum(jnp.sum(blk_start[:, None] >= pends[None, :], axis=1), N_EXPERTS - 1).astype(jnp.int32)
    n_used = (pends[-1:] // MOE_BLOCK).astype(jnp.int32)
    perm = jnp.argsort(inv_perm)
    pstart_col = pstarts[perm].astype(jnp.float32)[:, None]
    dest, gates = _dest(gate_dense, rank_dense, pstart_col)
    xs = _dispatch(xp, dest, (pstarts + cnt).astype(jnp.int32), (padded - cnt).astype(jnp.int32), n_used, n_rows)
    ys = _experts(xs.reshape(n_rows * ROW_SLAB, 128), block_e, n_used, w_gate, w_up, w_down, layer)
    return _combine(x, dest, gates, ys.reshape(n_rows, ROW_SLAB, 128), ws_gate, ws_up, ws_down, ln_g, ln_b)


def kernel(x, mem, w_in_a, sinks, w_in_b, lambda_qk, subln_g, w_mem_kv, w_out, ln_g, ln_b, w_router,
           router_bias, w_gate, w_up, w_down, w_shared_gate, w_shared_up, w_shared_down):
    batch, seq, d = x.shape
    n = batch * seq
    cdt = MXU_DTYPE
    qa, kva = SWA_Q_HEADS * HEAD_DIM, SWA_KV_HEADS * HEAD_DIM
    qb, vb = DIFF_HEADS * 2 * HEAD_DIM, DIFF_HEADS * DIFF_V_DIM

    rows = np.arange(N_EXPERTS)
    perm = (rows % N_GROUPS) * PER_GROUP + rows // N_GROUPS
    inv_perm = jnp.asarray(np.argsort(perm), jnp.int32)
    eid_col = jnp.asarray(perm[:, None], jnp.int32)

    xf = x.reshape(n, d)
    kv_all = _mem_kv(mem.reshape(-1, d), w_mem_kv)
    mlen = mem.shape[1]
    kv_all = kv_all.reshape(DEPTH * batch, mlen, kv_all.shape[-1])

    def in_proj_of(i):
        j = i // 2
        if i % 2 == 0:
            w = w_in_a[j].astype(cdt)
            return ([w[:, :qa], w[:, qa:qa + kva], w[:, qa + kva:qa + 2 * kva], w[:, qa + 2 * kva:]],
                    (QK_SCALE, 1.0, 1.0, QK_SCALE))
        w = w_in_b[j].astype(cdt)
        return ([w[:, :qb], w[:, qb:2 * qb], w[:, 2 * qb:2 * qb + vb], w[:, 2 * qb + vb:]],
                (QK_SCALE * LOG2E, 1.0, 1.0, QK_SCALE))

    for i in range(DEPTH):
        j = i // 2
        q, k, v, qm = _project(xf, *in_proj_of(i))
        if i % 2 == 0:
            mix = _swa_attention(q, k, v, sinks[j], batch, seq)
        else:
            lam_init = 0.8 - 0.6 * math.exp(-0.3 * i)
            mix = _diff_attention(q, k, v, lambda_qk[j], subln_g[j][None, :], batch, seq, lam_init)
        kv_l = kv_all[i * batch:(i + 1) * batch]
        xf, xp = _out_proj(mix, qm, kv_l, w_out[i].astype(cdt), xf, ln_g[i, 0][None, :], ln_b[i, 0][None, :],
                           batch, seq)
        w_router_t = jnp.transpose(w_router[i])[perm]
        bias_col = router_bias[i][perm][:, None]
        xf = _moe(xf, xp, w_router_t, bias_col, eid_col, inv_perm,
                  w_gate, w_up, w_down, i,
                  w_shared_gate[i].astype(cdt), w_shared_up[i].astype(cdt), w_shared_down[i].astype(cdt),
                  ln_g[i, 1][None, :], ln_b[i, 1][None, :])
    return xf.reshape(batch, seq, d)
```
